```python
import math
import jax, jax.numpy as jnp
from jax import lax
import numpy as np

D_MODEL = 1024
BATCH = 8
SEQ = 4096
DEPTH = 4

GRID_W = 64
CTX_LEN = 256
N_MIXERS = 2
NORM_EPS = 1e-6

DN_QK_HEADS = 8
DN_V_HEADS = 16
DN_HEAD_K = 128
DN_HEAD_V = 128
DN_KEY_DIM = DN_QK_HEADS * DN_HEAD_K
DN_VAL_DIM = DN_V_HEADS * DN_HEAD_V
DN_CONV_DIM = 2 * DN_KEY_DIM + DN_VAL_DIM
DN_CONV_K = 5
DN_CHUNK = 64
DN_IN_DIM = DN_CONV_DIM + DN_VAL_DIM + 4 * DN_V_HEADS
DT_MIN = 0.001
DT_MAX = 0.1

ATT_Q_HEADS = 8
ATT_KV_HEADS = 2
ATT_HEAD_DIM = 128
ATT_Q_DIM = ATT_Q_HEADS * ATT_HEAD_DIM
ATT_KV_DIM = ATT_KV_HEADS * ATT_HEAD_DIM
ATT_IN_DIM = ATT_Q_DIM + 2 * ATT_KV_DIM + ATT_Q_DIM
ATT_BLOCK = 128
ROPE_THETA = 10000.0

N_DN_LAYERS = (DEPTH + N_MIXERS - 1) // N_MIXERS
N_ATT_LAYERS = DEPTH // N_MIXERS

kernel_name = 'hybrid_deltanet_gqa_dit'

F32 = jnp.float32


def rms_norm(x, g):
    xf = x.astype(F32)
    y = xf * lax.rsqrt(jnp.mean(xf * xf, axis=-1, keepdims=True) + NORM_EPS)
    return (y * g.astype(F32)).astype(x.dtype)


def l2_norm(x):
    xf = x.astype(F32)
    return (xf * lax.rsqrt(jnp.sum(xf * xf, axis=-1, keepdims=True) + NORM_EPS)).astype(x.dtype)


def centred_dwconv(x, w):
    k = w.shape[0]
    return lax.conv_general_dilated(x, w[:, None, :].astype(x.dtype), window_strides=(1,),
                                    padding=[(k // 2, k // 2)],
                                    dimension_numbers=('NWC', 'WIO', 'NWC'),
                                    feature_group_count=x.shape[-1])


def chunk_gated_delta(q, k, v, g, beta, s0):
    b, h, l, dk = k.shape
    dv = v.shape[-1]
    n = l // DN_CHUNK
    q = q.astype(F32) * (dk ** -0.5)
    k = k.astype(F32)
    v32 = v.astype(F32)
    ch = lambda t: t.reshape(b, h, n, DN_CHUNK, *t.shape[3:])
    qc, kc, vc = ch(q), ch(k), ch(v32)
    gc = jnp.cumsum(ch(g.astype(F32)), axis=-1)
    bc = ch(beta.astype(F32))
    idx = jnp.arange(DN_CHUNK)
    lower = idx[:, None] >= idx[None, :]
    strict = idx[:, None] > idx[None, :]
    decay = jnp.exp(jnp.where(lower, gc[..., :, None] - gc[..., None, :], -jnp.inf))
    kb = kc * bc[..., None]
    vb = vc * bc[..., None]
    a = jnp.where(strict, jnp.einsum('bhncd,bhnsd->bhncs', kb, kc) * decay, 0.0)
    eye = jnp.eye(DN_CHUNK, dtype=F32)
    t_inv = lax.linalg.triangular_solve(eye + a, jnp.broadcast_to(eye, a.shape),
                                        left_side=True, lower=True, unit_diagonal=True)
    u = jnp.einsum('bhncs,bhnse->bhnce', t_inv, vb)
    w = jnp.einsum('bhncs,bhnsd->bhncd', t_inv, kb * jnp.exp(gc)[..., None])
    qk = jnp.einsum('bhncd,bhnsd->bhncs', qc, kc) * decay
    q_dec = qc * jnp.exp(gc)[..., None]
    k_dec = kc * jnp.exp(gc[..., -1:] - gc)[..., None]
    g_last = jnp.exp(gc[..., -1])

    def step(s, xs):
        qk_i, qd_i, kd_i, u_i, w_i, gl_i = xs
        v_new = u_i - jnp.einsum('bhcd,bhde->bhce', w_i, s)
        o = jnp.einsum('bhcd,bhde->bhce', qd_i, s) + jnp.einsum('bhcs,bhse->bhce', qk_i, v_new)
        s = s * gl_i[..., None, None] + jnp.einsum('bhcd,bhce->bhde', kd_i, v_new)
        return s, o

    xs = tuple(jnp.moveaxis(t_, 2, 0) for t_ in (qk, q_dec, k_dec, u, w, g_last))
    s_fin, o = lax.scan(step, s0.astype(F32), xs)
    o = jnp.moveaxis(o, 0, 2).reshape(b, h, l, dv)
    return o.astype(v.dtype), s_fin


def bidir_delta(q, k, v, g, beta, s0_f, s0_b):
    o_f, s_f = chunk_gated_delta(q, k, v, g[:, 0], beta[:, 0], s0_f)
    fl = lambda t: jnp.flip(t, axis=2)
    o_b, s_b = chunk_gated_delta(fl(q), fl(k), fl(v), fl(g[:, 1]), fl(beta[:, 1]), s0_b)
    return o_f + fl(o_b), s_f, s_b


def deltanet_mixer(u_lat, u_ctx, w_in, conv_w, a_log, dt_bias, o_norm_g, w_out, need_ctx):
    def branch_inputs(u):
        b, l, _ = u.shape
        p = u @ w_in
        qkv = jax.nn.silu(centred_dwconv(p[..., :DN_CONV_DIM], conv_w))
        z = p[..., DN_CONV_DIM:DN_CONV_DIM + DN_VAL_DIM]
        ab = p[..., DN_CONV_DIM + DN_VAL_DIM:].reshape(b, l, 2, 2, DN_V_HEADS).astype(F32)
        q, k, v = jnp.split(qkv, [DN_KEY_DIM, 2 * DN_KEY_DIM], axis=-1)
        rep = DN_V_HEADS // DN_QK_HEADS
        q = jnp.repeat(l2_norm(q.reshape(b, l, DN_QK_HEADS, DN_HEAD_K)), rep, axis=2)
        k = jnp.repeat(l2_norm(k.reshape(b, l, DN_QK_HEADS, DN_HEAD_K)), rep, axis=2)
        v = v.reshape(b, l, DN_V_HEADS, DN_HEAD_V)
        beta = jax.nn.sigmoid(ab[:, :, 0])
        g = -jnp.exp(a_log.astype(F32)) * jax.nn.softplus(ab[:, :, 1] + dt_bias.astype(F32))
        bhl = lambda t: jnp.transpose(t, (0, 2, 1, 3))
        return bhl(q), bhl(k), bhl(v), jnp.transpose(g, (0, 2, 3, 1)), jnp.transpose(beta, (0, 2, 3, 1)), z

    def finish(o, z):
        b, h, l, dv = o.shape
        o = rms_norm(jnp.transpose(o, (0, 2, 1, 3)), o_norm_g).astype(z.dtype)
        return (o.reshape(b, l, DN_VAL_DIM) * jax.nn.silu(z)) @ w_out

    qc, kc, vc, gc, bc, zc = branch_inputs(u_ctx)
    s_zero = jnp.zeros((u_ctx.shape[0], DN_V_HEADS, DN_HEAD_K, DN_HEAD_V), F32)
    o_ctx, s_f, s_b = bidir_delta(qc, kc, vc, gc, bc, s_zero, s_zero)
    ql, kl, vl, gl, bl, zl = branch_inputs(u_lat)
    o_lat, _, _ = bidir_delta(ql, kl, vl, gl, bl, s_f, s_b)
    y_lat = finish(o_lat, zl)
    y_ctx = finish(o_ctx, zc) if need_ctx else None
    return y_lat, y_ctx


def axial_rope_angles(n):
    rows = n // GRID_W
    row = jnp.broadcast_to(jnp.arange(rows)[:, None], (rows, GRID_W)).reshape(-1).astype(F32)
    col = jnp.broadcast_to(jnp.arange(GRID_W)[None, :], (rows, GRID_W)).reshape(-1).astype(F32)
    axis_dim = ATT_HEAD_DIM // 2
    inv = ROPE_THETA ** (-jnp.arange(0, axis_dim, 2, dtype=F32) / axis_dim)
    return row[:, None] * inv, col[:, None] * inv


def rope_rotate(x, ang):
    x1, x2 = jnp.split(x, 2, axis=-1)
    cos = jnp.cos(ang)[:, None, :].astype(x.dtype)
    sin = jnp.sin(ang)[:, None, :].astype(x.dtype)
    return jnp.concatenate([x1 * cos - x2 * sin, x2 * cos + x1 * sin], axis=-1)


def apply_axial_rope(x, ang_r, ang_c):
    xr, xc = jnp.split(x, 2, axis=-1)
    return jnp.concatenate([rope_rotate(xr, ang_r), rope_rotate(xc, ang_c)], axis=-1)


def gqa_attend(q, k, v):
    b, lq, hq, d = q.shape
    qg = q.reshape(b, lq, ATT_KV_HEADS, hq // ATT_KV_HEADS, d)
    s = jnp.einsum('bqhgd,bkhd->bhgqk', qg, k).astype(F32) * (d ** -0.5)
    p = jax.nn.softmax(s, axis=-1).astype(v.dtype)
    return jnp.einsum('bhgqk,bkhd->bqhgd', p, v).reshape(b, lq, hq, d)


def blocked_gqa(q, k, v):
    b, lq, hq, d = q.shape
    nb = lq // ATT_BLOCK
    qb = jnp.moveaxis(q.reshape(b, nb, ATT_BLOCK, hq, d), 1, 0)
    ob = lax.map(lambda qi: gqa_attend(qi, k, v), qb)
    return jnp.moveaxis(ob, 0, 1).reshape(b, lq, hq, d)


def attention_mixer(u_lat, u_ctx, w_in, q_norm_g, k_norm_g, w_out, need_ctx):
    def project(u):
        b, l, _ = u.shape
        p = u @ w_in
        q, k, v, z = jnp.split(p, [ATT_Q_DIM, ATT_Q_DIM + ATT_KV_DIM, ATT_Q_DIM + 2 * ATT_KV_DIM], axis=-1)
        q = rms_norm(q.reshape(b, l, ATT_Q_HEADS, ATT_HEAD_DIM), q_norm_g)
        k = rms_norm(k.reshape(b, l, ATT_KV_HEADS, ATT_HEAD_DIM), k_norm_g)
        v = v.reshape(b, l, ATT_KV_HEADS, ATT_HEAD_DIM)
        return q, k, v, z

    def finish(o, z):
        b, l = o.shape[:2]
        return (o.reshape(b, l, ATT_Q_DIM) * jax.nn.silu(z)) @ w_out

    qc, kc, vc, zc = project(u_ctx)
    ql, kl, vl, zl = project(u_lat)
    ang_r, ang_c = axial_rope_angles(u_lat.shape[1])
    ql = apply_axial_rope(ql, ang_r, ang_c)
    kl = apply_axial_rope(kl, ang_r, ang_c)
    k_all = jnp.concatenate([kc, kl], axis=1)
    v_all = jnp.concatenate([vc, vl], axis=1)
    y_lat = finish(blocked_gqa(ql, k_all, v_all), zl)
    y_ctx = finish(gqa_attend(qc, kc, vc), zc) if need_ctx else None
    return y_lat, y_ctx


def setup_inputs(seed: int = 0) -> dict:
    key = jax.random.key(seed)
    ks = jax.random.split(key, 20)
    nrm = lambda k, shape, scale: jax.random.normal(k, shape, F32) * scale
    d = D_MODEL
    n_dn, n_att = N_DN_LAYERS, N_ATT_LAYERS
    a_log = jnp.log(jax.random.uniform(ks[8], (n_dn, 2, DN_V_HEADS), F32, 1.0, 16.0))
    dt = jnp.exp(jax.random.uniform(ks[9], (n_dn, 2, DN_V_HEADS), F32, math.log(DT_MIN), math.log(DT_MAX)))
    dt_bias = dt + jnp.log(-jnp.expm1(-dt))
    return {
        'x': nrm(ks[0], (BATCH, SEQ, d), 1.0),
        'c': nrm(ks[1], (BATCH, d), 1.0),
        'ctx': nrm(ks[2], (BATCH, CTX_LEN, d), 1.0),
        'c_ctx': nrm(ks[3], (d,), 1.0),
        'norm_g': 1.0 + nrm(ks[4], (DEPTH, d), 0.02),
        'ada_w': nrm(ks[5], (DEPTH, d, 3 * d), 0.5 * d ** -0.5),
        'ada_b': nrm(ks[6], (DEPTH, 3 * d), 0.02),
        'dn_w_in': nrm(ks[7], (n_dn, d, DN_IN_DIM), d ** -0.5),
        'dn_conv_w': nrm(ks[10], (n_dn, DN_CONV_K, DN_CONV_DIM), DN_CONV_K ** -0.5),
        'dn_a_log': a_log,
        'dn_dt_bias': dt_bias,
        'dn_o_norm_g': 1.0 + nrm(ks[11], (n_dn, DN_HEAD_V), 0.02),
        'dn_w_out': nrm(ks[12], (n_dn, DN_VAL_DIM, d), DN_VAL_DIM ** -0.5),
        'att_w_in': nrm(ks[13], (n_att, d, ATT_IN_DIM), d ** -0.5),
        'att_q_norm_g': 1.0 + nrm(ks[14], (n_att, ATT_HEAD_DIM), 0.02),
        'att_k_norm_g': 1.0 + nrm(ks[15], (n_att, ATT_HEAD_DIM), 0.02),
        'att_w_out': nrm(ks[16], (n_att, ATT_Q_DIM, d), ATT_Q_DIM ** -0.5),
        'final_norm_g': 1.0 + nrm(ks[17], (d,), 0.02),
    }


def reference(x, c, ctx, c_ctx, norm_g, ada_w, ada_b, dn_w_in, dn_conv_w, dn_a_log, dn_dt_bias,
              dn_o_norm_g, dn_w_out, att_w_in, att_q_norm_g, att_k_norm_g, att_w_out, final_norm_g):
    h_lat, h_ctx = x, ctx
    sc_lat, sc_ctx = jax.nn.silu(c), jax.nn.silu(c_ctx)
    for i in range(DEPTH):
        need_ctx = i < DEPTH - 1
        shift, scale, gate = jnp.split(sc_lat @ ada_w[i] + ada_b[i], 3, axis=-1)
        shift_c, scale_c, gate_c = jnp.split(sc_ctx @ ada_w[i] + ada_b[i], 3, axis=-1)
        u_lat = rms_norm(h_lat, norm_g[i]) * (1.0 + scale[:, None]) + shift[:, None]
        u_ctx = rms_norm(h_ctx, norm_g[i]) * (1.0 + scale_c) + shift_c
        j = i // N_MIXERS
        if i % N_MIXERS == 0:
            y_lat, y_ctx = deltanet_mixer(u_lat, u_ctx, dn_w_in[j], dn_conv_w[j], dn_a_log[j], dn_dt_bias[j],
                                          dn_o_norm_g[j], dn_w_out[j], need_ctx)
        else:
            y_lat, y_ctx = attention_mixer(u_lat, u_ctx, att_w_in[j], att_q_norm_g[j], att_k_norm_g[j],
                                           att_w_out[j], need_ctx)
        h_lat = h_lat + gate[:, None] * y_lat
        if need_ctx:
            h_ctx = h_ctx + gate_c * y_ctx
    return rms_norm(h_lat, final_norm_g)
```

```python
import functools
import math

import jax
import jax.numpy as jnp
from jax import lax
from jax.experimental import pallas as pl
from jax.experimental.pallas import tpu as pltpu

F32 = jnp.float32
BF16 = jnp.bfloat16

NORM_EPS = 1e-6
GRID_W = 64
ROPE_THETA = 10000.0

HEAD = 128
DN_QK_HEADS = 8
DN_V_HEADS = 16
DN_KEY_DIM = DN_QK_HEADS * HEAD
DN_VAL_DIM = DN_V_HEADS * HEAD
DN_CONV_DIM = 2 * DN_KEY_DIM + DN_VAL_DIM
DN_CONV_K = 5
DN_CHUNK = 64
ATT_Q_HEADS = 8
ATT_KV_HEADS = 2
ATT_GROUP = ATT_Q_HEADS // ATT_KV_HEADS
ATT_Q_DIM = ATT_Q_HEADS * HEAD
ATT_KV_DIM = ATT_KV_HEADS * HEAD

ROW_TILE = 256
LANES = 128
HALO = 8
MOD_ROWS = 16
N_COL_CHUNK = 512
VMEM_LIMIT = 56 * 1024 * 1024
NEG_BIG = -1e30


def _cparams(*sem):
    return pltpu.CompilerParams(dimension_semantics=sem, vmem_limit_bytes=VMEM_LIMIT)


def _resident(shape):
    return pl.BlockSpec(shape, lambda *_: (0,) * len(shape), pipeline_mode=pl.Buffered(1))


def _silu(x):
    return x * jax.nn.sigmoid(x)


def _mod_body(c_ref, w_ref, b_ref, o_ref):
    sc = _silu(c_ref[...])
    o_ref[...] = jnp.dot(sc, w_ref[...], preferred_element_type=F32) + b_ref[...]


def _modulation(cvec, ada_w, ada_b):
    depth, d, d3 = ada_w.shape
    tn = 1024
    return pl.pallas_call(
        _mod_body,
        grid=(depth, d3 // tn),
        in_specs=[
            pl.BlockSpec((MOD_ROWS, d), lambda i, j: (0, 0)),
            pl.BlockSpec((None, d, tn), lambda i, j: (i, 0, j)),
            pl.BlockSpec((None, 1, tn), lambda i, j: (i, 0, j)),
        ],
        out_specs=pl.BlockSpec((None, MOD_ROWS, tn), lambda i, j: (i, 0, j)),
        out_shape=jax.ShapeDtypeStruct((depth, MOD_ROWS, d3), F32),
        compiler_params=_cparams("arbitrary", "arbitrary"),
        name="adaln_modulation",
    )(cvec, ada_w, ada_b.reshape(depth, 1, d3))


def _mod_spec(layer, batch, d3):
    return pl.BlockSpec((None, None, 1, d3),
                        lambda b, r: (layer, jnp.where(r == 0, batch, b), 0, 0))


def _modulated_norm(x, g, mod, d):
    ms = jnp.mean(x * x, axis=-1, keepdims=True)
    xn = x * lax.rsqrt(ms + NORM_EPS) * g
    return xn * (1.0 + mod[:, d:2 * d]) + mod[:, 0:d]


def _dn_inproj_body(h_ref, mod_ref, g_ref, w_ref, wab_ref, qkv_ref, z_ref, ab_ref):
    d = h_ref.shape[-1]
    u = _modulated_norm(h_ref[...], g_ref[...], mod_ref[...], d).astype(BF16)
    for n in range(DN_CONV_DIM // N_COL_CHUNK):
        cs = slice(n * N_COL_CHUNK, (n + 1) * N_COL_CHUNK)
        qkv_ref[:, cs] = jnp.dot(u, w_ref[:, cs], preferred_element_type=F32)
    for n in range(DN_VAL_DIM // N_COL_CHUNK):
        cs = slice(n * N_COL_CHUNK, (n + 1) * N_COL_CHUNK)
        ws = slice(DN_CONV_DIM + n * N_COL_CHUNK, DN_CONV_DIM + (n + 1) * N_COL_CHUNK)
        z_ref[:, cs] = jnp.dot(u, w_ref[:, ws], preferred_element_type=F32).astype(BF16)
    ab_ref[...] = jnp.dot(u, wab_ref[...], preferred_element_type=F32)


def _dn_inproj(h, mod4, norm_g, w_main, w_ab, layer):
    b, lt, d = h.shape
    nt = lt // ROW_TILE
    d3 = mod4.shape[-1]
    row = lambda c: pl.BlockSpec((None, ROW_TILE, c), lambda bi, r: (bi, r, 0))
    return pl.pallas_call(
        _dn_inproj_body,
        grid=(b, nt),
        in_specs=[row(d), _mod_spec(layer, b, d3), _resident((1, d)),
                  _resident(w_main.shape), _resident(w_ab.shape)],
        out_specs=[row(DN_CONV_DIM), row(DN_VAL_DIM), row(LANES)],
        out_shape=[jax.ShapeDtypeStruct((b, lt, DN_CONV_DIM), F32),
                   jax.ShapeDtypeStruct((b, lt, DN_VAL_DIM), BF16),
                   jax.ShapeDtypeStruct((b, lt, LANES), F32)],
        compiler_params=_cparams("arbitrary", "arbitrary"),
        name="dn_inproj",
    )(h, mod4, norm_g, w_main, w_ab)


def _dn_conv_body(x_ref, prev_ref, next_ref, cw_ref, ab_ref, alog_ref, dtb_ref,
                  qkv_ref, gate_ref, pad_ref):
    r = pl.program_id(1)
    nt = pl.num_programs(1)
    has_prev = r > 1
    has_next = jnp.logical_and(r > 0, r < nt - 1)
    half = DN_CONV_K // 2
    for hb in range(DN_CONV_DIM // HEAD):
        cs = slice(hb * HEAD, (hb + 1) * HEAD)
        pad_ref[0:HALO, :] = jnp.where(has_prev, prev_ref[:, cs], 0.0)
        pad_ref[HALO:HALO + ROW_TILE, :] = x_ref[:, cs]
        pad_ref[HALO + ROW_TILE:, :] = jnp.where(has_next, next_ref[:, cs], 0.0)
        acc = None
        for j in range(DN_CONV_K):
            win = pad_ref[pl.ds(HALO - half + j, ROW_TILE), :]
            term = win * cw_ref[j:j + 1, cs]
            acc = term if acc is None else acc + term
        y = _silu(acc)
        if hb < 2 * DN_QK_HEADS:
            y = y * lax.rsqrt(jnp.sum(y * y, axis=-1, keepdims=True) + NORM_EPS)
        qkv_ref[:, cs] = y.astype(BF16)
    ab = ab_ref[...]
    lane = lax.broadcasted_iota(jnp.int32, ab.shape, 1)
    beta = jax.nn.sigmoid(ab)
    g = -jnp.exp(alog_ref[...]) * jax.nn.softplus(ab + dtb_ref[...])
    gate_ref[...] = jnp.where(lane < 2 * DN_V_HEADS, beta, g)


def _dn_conv(qkv_pre, ab, conv_w, alog_vec, dtb_vec):
    b, lt, c = qkv_pre.shape
    nt = lt // ROW_TILE
    per = ROW_TILE // HALO
    nhalo = lt // HALO
    row = lambda w: pl.BlockSpec((None, ROW_TILE, w), lambda bi, r: (bi, r, 0))
    return pl.pallas_call(
        _dn_conv_body,
        grid=(b, nt),
        in_specs=[
            row(c),
            pl.BlockSpec((None, HALO, c), lambda bi, r: (bi, jnp.maximum(r * per - 1, 0), 0)),
            pl.BlockSpec((None, HALO, c), lambda bi, r: (bi, jnp.minimum((r + 1) * per, nhalo - 1), 0)),
            _resident(conv_w.shape), row(LANES), _resident((1, LANES)), _resident((1, LANES)),
        ],
        out_specs=[row(c), row(LANES)],
        out_shape=[jax.ShapeDtypeStruct((b, lt, c), BF16), jax.ShapeDtypeStruct((b, lt, LANES), F32)],
        scratch_shapes=[pltpu.VMEM((ROW_TILE + 2 * HALO, HEAD), F32)],
        compiler_params=_cparams("arbitrary", "arbitrary"),
        name="dn_conv_gates",
    )(qkv_pre, qkv_pre, qkv_pre, conv_w, ab, alog_vec, dtb_vec)


def _dot(a, b):
    return jnp.dot(a.astype(BF16), b.astype(BF16), preferred_element_type=F32)


def _split(a):
    hi = a.astype(BF16)
    return hi, (a - hi.astype(F32)).astype(BF16)


def _dot_split(a, b):
    a_hi, a_lo = _split(a)
    b_hi, b_lo = _split(b)
    mm = lambda x, y: jnp.dot(x, y, preferred_element_type=F32)
    return mm(a_hi, b_hi) + (mm(a_lo, b_hi) + mm(a_hi, b_lo))


def _dot_nt(a, b):
    return lax.dot_general(a.astype(BF16), b.astype(BF16), (((1,), (1,)), ((), ())),
                           preferred_element_type=F32)


def _dot_tn(a, b):
    return lax.dot_general(a.astype(BF16), b.astype(BF16), (((0,), (0,)), ((), ())),
                           preferred_element_type=F32)


def _dot_nt_exact(a, b):
    return lax.dot_general(a, b, (((1,), (1,)), ((), ())), precision=lax.Precision.HIGHEST,
                           preferred_element_type=F32)


def _delta_body(q_ref, k_ref, v_ref, gr_ref, o_ref, s_ref, *, reverse):
    c = DN_CHUNK
    n_chunks = ROW_TILE // c
    rep = DN_V_HEADS // DN_QK_HEADS

    @pl.when(pl.program_id(2) == 0)
    def _():
        s_ref[...] = jnp.zeros_like(s_ref)

    ti = lax.broadcasted_iota(jnp.int32, (ROW_TILE, ROW_TILE), 0)
    tj = lax.broadcasted_iota(jnp.int32, (ROW_TILE, ROW_TILE), 1)
    same = (ti // c) == (tj // c)
    before = (tj >= ti) if reverse else (tj <= ti)
    cum_mat = jnp.where(jnp.logical_and(same, before), 1.0, 0.0)
    eye_mat = jnp.where(ti == tj, 1.0, 0.0)
    gr = gr_ref[...]
    gr8 = jnp.concatenate([gr, jnp.zeros((8 - gr.shape[0], ROW_TILE), F32)], axis=0)
    cum_col = _dot_nt_exact(cum_mat, gr8)
    cum_row = _dot_nt_exact(gr8, cum_mat)
    raw_col = _dot_nt_exact(eye_mat, gr8)

    ci = lax.broadcasted_iota(jnp.int32, (c, c), 0)
    cj = lax.broadcasted_iota(jnp.int32, (c, c), 1)
    incl = (cj >= ci) if reverse else (cj <= ci)
    strict = (cj > ci) if reverse else (cj < ci)
    eye_c = jnp.where(ci == cj, 1.0, 0.0)
    last = 0 if reverse else c - 1
    q_scale = HEAD ** -0.5

    order = range(n_chunks - 1, -1, -1) if reverse else range(n_chunks)
    for ch in order:
        rows = slice(ch * c, (ch + 1) * c)
        qc = q_ref[rows, :]
        kc = k_ref[rows, :]
        kk = _dot_nt(kc, kc)
        qk = _dot_nt(qc, kc)
        kf = kc.astype(F32)
        qf = qc.astype(F32)
        for s in range(rep):
            gcol = cum_col[rows, s:s + 1]
            grow = cum_row[s:s + 1, rows]
            bcol = raw_col[rows, rep + s:rep + s + 1]
            gtot = gcol[last:last + 1, :]
            decay = jnp.exp(jnp.where(incl, gcol - grow, NEG_BIG))
            a = jnp.where(strict, bcol * kk * decay, 0.0)
            t_inv = eye_c - a
            x = a
            for _ in range(int(math.log2(c)) - 1):
                x = _dot_split(x, x)
                t_inv = t_inv + _dot_split(t_inv, x)
            eg = jnp.exp(gcol)
            vf = v_ref[rows, s * HEAD:(s + 1) * HEAD].astype(F32)
            rhs = jnp.concatenate([vf * bcol, kf * (bcol * eg)], axis=1)
            uw = _dot(t_inv, rhs)
            u = uw[:, :HEAD]
            w = uw[:, HEAD:]
            state = s_ref[s]
            ws = _dot(jnp.concatenate([w, qf * (q_scale * eg)], axis=0), state)
            v_new = u - ws[:c]
            qkm = q_scale * qk * decay
            o = ws[c:] + _dot(qkm, v_new)
            k_dec = kf * jnp.exp(gtot - gcol)
            s_ref[s] = state * jnp.exp(gtot) + _dot_tn(k_dec, v_new)
            o_ref[rows, s * HEAD:(s + 1) * HEAD] = o


def _delta_rule(qkv, gate_rows, reverse):
    b, lt, _ = qkv.shape
    nt = lt // ROW_TILE
    rep = DN_V_HEADS // DN_QK_HEADS
    d = 1 if reverse else 0
    if reverse:
        tile = lambda r: jnp.where(r == 0, 0, nt - r)
    else:
        tile = lambda r: r
    return pl.pallas_call(
        functools.partial(_delta_body, reverse=reverse),
        grid=(b, DN_QK_HEADS, nt),
        in_specs=[
            pl.BlockSpec((None, ROW_TILE, HEAD), lambda bi, j, r: (bi, tile(r), j)),
            pl.BlockSpec((None, ROW_TILE, HEAD), lambda bi, j, r: (bi, tile(r), DN_QK_HEADS + j)),
            pl.BlockSpec((None, ROW_TILE, rep * HEAD),
                         lambda bi, j, r: (bi, tile(r), 2 * DN_KEY_DIM // (rep * HEAD) + j)),
            pl.BlockSpec((None, None, None, 2 * rep, ROW_TILE), lambda bi, j, r: (bi, d, j, 0, tile(r))),
        ],
        out_specs=pl.BlockSpec((None, ROW_TILE, rep * HEAD), lambda bi, j, r: (bi, tile(r), j)),
        out_shape=jax.ShapeDtypeStruct((b, lt, DN_VAL_DIM), F32),
        scratch_shapes=[pltpu.VMEM((rep, HEAD, HEAD), F32)],
        compiler_params=_cparams("arbitrary", "arbitrary", "arbitrary"),
        name="dn_delta_bwd" if reverse else "dn_delta_fwd",
    )(qkv, qkv, qkv, gate_rows)


def _outproj_body(*refs, n_o, head_norm, final_norm):
    o_refs = refs[:n_o]
    z_ref, h_ref, mod_ref, w_ref = refs[n_o:n_o + 4]
    rest = list(refs[n_o + 4:])
    og_ref = rest.pop(0) if head_norm else None
    fg_ref = rest.pop(0) if final_norm else None
    out_ref = rest.pop(0)
    d = h_ref.shape[-1]
    width = z_ref.shape[-1]
    parts = []
    for hb in range(width // HEAD):
        cs = slice(hb * HEAD, (hb + 1) * HEAD)
        o = o_refs[0][:, cs].astype(F32)
        for extra in o_refs[1:]:
            o = o + extra[:, cs].astype(F32)
        if head_norm:
            o = o * lax.rsqrt(jnp.mean(o * o, axis=-1, keepdims=True) + NORM_EPS) * og_ref[...]
        parts.append((o * _silu(z_ref[:, cs].astype(F32))).astype(BF16))
    y = jnp.dot(jnp.concatenate(parts, axis=1), w_ref[...], preferred_element_type=F32)
    hn = h_ref[...] + mod_ref[:, 2 * d:3 * d] * y
    if final_norm:
        hn = hn * lax.rsqrt(jnp.mean(hn * hn, axis=-1, keepdims=True) + NORM_EPS) * fg_ref[...]
    out_ref[...] = hn


def _outproj(o_list, z, h, mod4, w_out, layer, head_g=None, final_g=None):
    b, lt, d = h.shape
    nt = lt // ROW_TILE
    d3 = mod4.shape[-1]
    width = z.shape[-1]
    skip = 1 if final_g is not None else 0
    row = lambda c: pl.BlockSpec((None, ROW_TILE, c), lambda bi, r: (bi, r + skip, 0))
    mod_spec = pl.BlockSpec((None, None, 1, d3),
                            lambda bi, r: (layer, jnp.where(r + skip == 0, b, bi), 0, 0))
    in_specs = [row(width)] * len(o_list) + [row(width), row(d), mod_spec, _resident(w_out.shape)]
    args = list(o_list) + [z, h, mod4, w_out]
    if head_g is not None:
        in_specs.append(_resident((1, HEAD)))
        args.append(head_g)
    if final_g is not None:
        in_specs.append(_resident((1, d)))
        args.append(final_g)
    return pl.pallas_call(
        functools.partial(_outproj_body, n_o=len(o_list), head_norm=head_g is not None,
                          final_norm=final_g is not None),
        grid=(b, nt - skip),
        in_specs=in_specs,
        out_specs=pl.BlockSpec((None, ROW_TILE, d), lambda bi, r: (bi, r, 0)),
        out_shape=jax.ShapeDtypeStruct((b, lt - skip * ROW_TILE, d), F32),
        compiler_params=_cparams("arbitrary", "arbitrary"),
        name="mixer_outproj",
    )(*args)


def _rope_partner(x):
    lane = lax.broadcasted_iota(jnp.int32, x.shape, 1)
    quarter = HEAD // 4
    return jnp.where((lane % (2 * quarter)) < quarter,
                     pltpu.roll(x, HEAD - quarter, axis=1), pltpu.roll(x, quarter, axis=1))


def _att_inproj_body(h_ref, mod_ref, g_ref, w_ref, qg_ref, kg_ref, cos_ref, sin_ref,
                     q_ref, k_ref, v_ref, z_ref):
    d = h_ref.shape[-1]
    u = _modulated_norm(h_ref[...], g_ref[...], mod_ref[...], d).astype(BF16)
    cos = cos_ref[...]
    sin = sin_ref[...]
    q_scale = (HEAD ** -0.5) * math.log2(math.e)

    def normed_rope(x, g):
        xn = x * lax.rsqrt(jnp.mean(x * x, axis=-1, keepdims=True) + NORM_EPS) * g
        return xn * cos + _rope_partner(xn) * sin

    for hb in range(ATT_Q_HEADS):
        cs = slice(hb * HEAD, (hb + 1) * HEAD)
        x = jnp.dot(u, w_ref[:, cs], preferred_element_type=F32)
        q_ref[:, cs] = (normed_rope(x, qg_ref[...]) * q_scale).astype(BF16)
    for hb in range(ATT_KV_HEADS):
        cs = slice(hb * HEAD, (hb + 1) * HEAD)
        ws = slice(ATT_Q_DIM + hb * HEAD, ATT_Q_DIM + (hb + 1) * HEAD)
        x = jnp.dot(u, w_ref[:, ws], preferred_element_type=F32)
        k_ref[:, cs] = normed_rope(x, kg_ref[...]).astype(BF16)
    v0 = ATT_Q_DIM + ATT_KV_DIM
    v_ref[...] = jnp.dot(u, w_ref[:, v0:v0 + ATT_KV_DIM], preferred_element_type=F32).astype(BF16)
    z0 = v0 + ATT_KV_DIM
    for n in range(ATT_Q_DIM // N_COL_CHUNK):
        cs = slice(n * N_COL_CHUNK, (n + 1) * N_COL_CHUNK)
        ws = slice(z0 + n * N_COL_CHUNK, z0 + (n + 1) * N_COL_CHUNK)
        z_ref[:, cs] = jnp.dot(u, w_ref[:, ws], preferred_element_type=F32).astype(BF16)


def _att_inproj(h, mod4, norm_g, w_in, q_g, k_g, cos_t, sin_t, layer):
    b, lt, d = h.shape
    nt = lt // ROW_TILE
    d3 = mod4.shape[-1]
    row = lambda c: pl.BlockSpec((None, ROW_TILE, c), lambda bi, r: (bi, r, 0))
    tab = pl.BlockSpec((ROW_TILE, HEAD), lambda bi, r: (r, 0))
    return pl.pallas_call(
        _att_inproj_body,
        grid=(b, nt),
        in_specs=[row(d), _mod_spec(layer, b, d3), _resident((1, d)), _resident(w_in.shape),
                  _resident((1, HEAD)), _resident((1, HEAD)), tab, tab],
        out_specs=[row(ATT_Q_DIM), row(ATT_KV_DIM), row(ATT_KV_DIM), row(ATT_Q_DIM)],
        out_shape=[jax.ShapeDtypeStruct((b, lt, ATT_Q_DIM), BF16),
                   jax.ShapeDtypeStruct((b, lt, ATT_KV_DIM), BF16),
                   jax.ShapeDtypeStruct((b, lt, ATT_KV_DIM), BF16),
                   jax.ShapeDtypeStruct((b, lt, ATT_Q_DIM), BF16)],
        compiler_params=_cparams("arbitrary", "arbitrary"),
        name="att_inproj",
    )(h, mod4, norm_g, w_in, q_g, k_g, cos_t, sin_t)


def _attn_body(q_ref, k_ref, v_ref, o_ref):
    r = pl.program_id(2)
    lt = k_ref.shape[0]
    n_kv = jnp.where(r == 0, 1, lt // ROW_TILE)
    for g in range(ATT_GROUP):
        cs = slice(g * HEAD, (g + 1) * HEAD)
        q = q_ref[:, cs]

        def step(i, carry):
            m, l, acc = carry
            off = pl.multiple_of(i * ROW_TILE, ROW_TILE)
            kc = k_ref[pl.ds(off, ROW_TILE), :]
            vc = v_ref[pl.ds(off, ROW_TILE), :]
            s = lax.dot_general(q, kc, (((1,), (1,)), ((), ())), preferred_element_type=F32)
            m_new = jnp.maximum(m, jnp.max(s, axis=-1, keepdims=True))
            alpha = jnp.exp2(m - m_new)
            p = jnp.exp2(s - m_new)
            l_new = alpha * l + jnp.sum(p, axis=-1, keepdims=True)
            acc_new = alpha * acc + jnp.dot(p.astype(BF16), vc, preferred_element_type=F32)
            return m_new, l_new, acc_new

        init = (jnp.full((ROW_TILE, 1), NEG_BIG, F32), jnp.zeros((ROW_TILE, 1), F32),
                jnp.zeros((ROW_TILE, HEAD), F32))
        m, l, acc = lax.fori_loop(0, n_kv, step, init)
        o_ref[:, cs] = (acc / l).astype(BF16)


def _attention(q, k, v):
    b, lt, _ = q.shape
    nt = lt // ROW_TILE
    gw = ATT_GROUP * HEAD
    return pl.pallas_call(
        _attn_body,
        grid=(b, ATT_KV_HEADS, nt),
        in_specs=[
            pl.BlockSpec((None, ROW_TILE, gw), lambda bi, hk, r: (bi, r, hk)),
            pl.BlockSpec((None, lt, HEAD), lambda bi, hk, r: (bi, 0, hk)),
            pl.BlockSpec((None, lt, HEAD), lambda bi, hk, r: (bi, 0, hk)),
        ],
        out_specs=pl.BlockSpec((None, ROW_TILE, gw), lambda bi, hk, r: (bi, r, hk)),
        out_shape=jax.ShapeDtypeStruct((b, lt, ATT_Q_DIM), BF16),
        compiler_params=_cparams("arbitrary", "arbitrary", "arbitrary"),
        name="gqa_attention",
    )(q, k, v)


def _rope_tables(n_lat, n_ctx):
    t = jnp.arange(n_lat)
    row = (t // GRID_W).astype(F32)
    col = (t % GRID_W).astype(F32)
    axis_dim = HEAD // 2
    inv = ROPE_THETA ** (-jnp.arange(0, axis_dim, 2, dtype=F32) / axis_dim)
    ang_r = row[:, None] * inv
    ang_c = col[:, None] * inv
    cos = jnp.concatenate([jnp.cos(ang_r)] * 2 + [jnp.cos(ang_c)] * 2, axis=-1)
    sin = jnp.concatenate([-jnp.sin(ang_r), jnp.sin(ang_r), -jnp.sin(ang_c), jnp.sin(ang_c)], axis=-1)
    cos = jnp.concatenate([jnp.ones((n_ctx, HEAD), F32), cos], axis=0)
    sin = jnp.concatenate([jnp.zeros((n_ctx, HEAD), F32), sin], axis=0)
    return cos, sin


def _lane_pad(vec, offset):
    out = jnp.zeros((1, LANES), F32)
    return lax.dynamic_update_slice(out, vec.reshape(1, -1).astype(F32), (0, offset))


def kernel(x, c, ctx, c_ctx, norm_g, ada_w, ada_b, dn_w_in, dn_conv_w, dn_a_log, dn_dt_bias,
           dn_o_norm_g, dn_w_out, att_w_in, att_q_norm_g, att_k_norm_g, att_w_out, final_norm_g):
    b, n_lat, d = x.shape
    n_ctx = ctx.shape[1]
    depth = norm_g.shape[0]
    assert n_ctx == ROW_TILE and n_lat % ROW_TILE == 0 and b < MOD_ROWS and d % LANES == 0
    lt = n_ctx + n_lat
    rep = DN_V_HEADS // DN_QK_HEADS

    cvec = jnp.concatenate([c, c_ctx[None, :], jnp.zeros((MOD_ROWS - b - 1, d), F32)], axis=0)
    mod4 = _modulation(cvec, ada_w, ada_b).reshape(depth, MOD_ROWS, 1, 3 * d)
    h = jnp.concatenate([ctx, x], axis=1)
    cos_t, sin_t = _rope_tables(n_lat, n_ctx)

    for i in range(depth):
        j = i // 2
        last = i == depth - 1
        g_i = norm_g[i].reshape(1, d)
        if i % 2 == 0:
            w_in = dn_w_in[j].astype(BF16)
            w_main = w_in[:, :DN_CONV_DIM + DN_VAL_DIM]
            w_ab = jnp.pad(w_in[:, DN_CONV_DIM + DN_VAL_DIM:], ((0, 0), (0, LANES - 4 * DN_V_HEADS)))
            qkv_pre, z, ab = _dn_inproj(h, mod4, g_i, w_main, w_ab, i)
            alog_vec = _lane_pad(dn_a_log[j], 2 * DN_V_HEADS)
            dtb_vec = _lane_pad(dn_dt_bias[j], 2 * DN_V_HEADS)
            qkv, gates = _dn_conv(qkv_pre, ab, dn_conv_w[j], alog_vec, dtb_vec)
            gt = gates[:, :, :4 * DN_V_HEADS].reshape(b, lt, 2, 2, DN_QK_HEADS, rep)
            gate_rows = jnp.transpose(gt[:, :, ::-1], (0, 3, 4, 2, 5, 1)).reshape(
                b, 2, DN_QK_HEADS, 2 * rep, lt)
            o_f = _delta_rule(qkv, gate_rows, reverse=False)
            o_b = _delta_rule(qkv, gate_rows, reverse=True)
            h = _outproj([o_f, o_b], z, h, mod4, dn_w_out[j].astype(BF16), i,
                         head_g=dn_o_norm_g[j].reshape(1, HEAD),
                         final_g=final_norm_g.reshape(1, d) if last else None)
        else:
            q, k, v, z = _att_inproj(h, mod4, g_i, att_w_in[j].astype(BF16),
                                     att_q_norm_g[j].reshape(1, HEAD), att_k_norm_g[j].reshape(1, HEAD),
                                     cos_t, sin_t, i)
            o = _attention(q, k, v)
            h = _outproj([o], z, h, mod4, att_w_out[j].astype(BF16), i,
                         final_g=final_norm_g.reshape(1, d) if last else None)
    return h
```

```python
import functools
import math

import jax
import jax.numpy as jnp
from jax import lax
from jax.experimental import pallas as pl
from jax.experimental.pallas import tpu as pltpu

F32 = jnp.float32
BF16 = jnp.bfloat16

NORM_EPS = 1e-6
GRID_W = 64
ROPE_THETA = 10000.0

HEAD = 128
DN_QK_HEADS = 8
DN_V_HEADS = 16
DN_KEY_DIM = DN_QK_HEADS * HEAD
DN_VAL_DIM = DN_V_HEADS * HEAD
DN_CONV_DIM = 2 * DN_KEY_DIM + DN_VAL_DIM
DN_CONV_K = 5
DN_CHUNK = 64
ATT_Q_HEADS = 8
ATT_KV_HEADS = 2
ATT_GROUP = ATT_Q_HEADS // ATT_KV_HEADS
ATT_Q_DIM = ATT_Q_HEADS * HEAD
ATT_KV_DIM = ATT_KV_HEADS * HEAD

ROW_TILE = 256
LANES = 128
HALO = 8
MOD_ROWS = 16
N_COL_CHUNK = 512
ATT_KV_UNROLL = 8
VMEM_LIMIT = 56 * 1024 * 1024
NEG_BIG = -1e30


def _cparams(*sem):
    return pltpu.CompilerParams(dimension_semantics=sem, vmem_limit_bytes=VMEM_LIMIT)


def _resident(shape):
    return pl.BlockSpec(shape, lambda *_: (0,) * len(shape), pipeline_mode=pl.Buffered(1))


def _silu(x):
    return x * jax.nn.sigmoid(x)


def _mod_body(c_ref, w_ref, b_ref, o_ref):
    sc = _silu(c_ref[...])
    o_ref[...] = jnp.dot(sc, w_ref[...], preferred_element_type=F32) + b_ref[...]


def _modulation(cvec, ada_w, ada_b):
    depth, d, d3 = ada_w.shape
    tn = 1024
    return pl.pallas_call(
        _mod_body,
        grid=(depth, d3 // tn),
        in_specs=[
            pl.BlockSpec((MOD_ROWS, d), lambda i, j: (0, 0)),
            pl.BlockSpec((None, d, tn), lambda i, j: (i, 0, j)),
            pl.BlockSpec((None, 1, tn), lambda i, j: (i, 0, j)),
        ],
        out_specs=pl.BlockSpec((None, MOD_ROWS, tn), lambda i, j: (i, 0, j)),
        out_shape=jax.ShapeDtypeStruct((depth, MOD_ROWS, d3), F32),
        compiler_params=_cparams("arbitrary", "arbitrary"),
        name="adaln_modulation",
    )(cvec, ada_w, ada_b.reshape(depth, 1, d3))


def _mod_spec(layer, batch, d3):
    return pl.BlockSpec((None, None, 1, d3),
                        lambda b, r: (layer, jnp.where(r == 0, batch, b), 0, 0))


def _modulated_norm(x, g, mod, d):
    ms = jnp.mean(x * x, axis=-1, keepdims=True)
    xn = x * lax.rsqrt(ms + NORM_EPS) * g
    return xn * (1.0 + mod[:, d:2 * d]) + mod[:, 0:d]


def _dn_inproj_body(h_ref, mod_ref, g_ref, w_ref, wab_ref, qkv_ref, z_ref, ab_ref):
    d = h_ref.shape[-1]
    u = _modulated_norm(h_ref[...], g_ref[...], mod_ref[...], d).astype(BF16)
    for n in range(DN_CONV_DIM // N_COL_CHUNK):
        cs = slice(n * N_COL_CHUNK, (n + 1) * N_COL_CHUNK)
        qkv_ref[:, cs] = jnp.dot(u, w_ref[:, cs], preferred_element_type=F32)
    for n in range(DN_VAL_DIM // N_COL_CHUNK):
        cs = slice(n * N_COL_CHUNK, (n + 1) * N_COL_CHUNK)
        ws = slice(DN_CONV_DIM + n * N_COL_CHUNK, DN_CONV_DIM + (n + 1) * N_COL_CHUNK)
        z_ref[:, cs] = jnp.dot(u, w_ref[:, ws], preferred_element_type=F32).astype(BF16)
    ab_ref[...] = jnp.dot(u, wab_ref[...], preferred_element_type=F32)


def _dn_inproj(h, mod4, norm_g, w_main, w_ab, layer):
    b, lt, d = h.shape
    nt = lt // ROW_TILE
    d3 = mod4.shape[-1]
    row = lambda c: pl.BlockSpec((None, ROW_TILE, c), lambda bi, r: (bi, r, 0))
    return pl.pallas_call(
        _dn_inproj_body,
        grid=(b, nt),
        in_specs=[row(d), _mod_spec(layer, b, d3), _resident((1, d)),
                  _resident(w_main.shape), _resident(w_ab.shape)],
        out_specs=[row(DN_CONV_DIM), row(DN_VAL_DIM), row(LANES)],
        out_shape=[jax.ShapeDtypeStruct((b, lt, DN_CONV_DIM), F32),
                   jax.ShapeDtypeStruct((b, lt, DN_VAL_DIM), BF16),
                   jax.ShapeDtypeStruct((b, lt, LANES), F32)],
        compiler_params=_cparams("arbitrary", "arbitrary"),
        name="dn_inproj",
    )(h, mod4, norm_g, w_main, w_ab)


def _dn_conv_body(x_ref, prev_ref, next_ref, cw_ref, ab_ref, alog_ref, dtb_ref,
                  qkv_ref, gate_ref, pad_ref):
    r = pl.program_id(1)
    nt = pl.num_programs(1)
    has_prev = r > 1
    has_next = jnp.logical_and(r > 0, r < nt - 1)
    half = DN_CONV_K // 2
    for hb in range(DN_CONV_DIM // HEAD):
        cs = slice(hb * HEAD, (hb + 1) * HEAD)
        pad_ref[0:HALO, :] = jnp.where(has_prev, prev_ref[:, cs], 0.0)
        pad_ref[HALO:HALO + ROW_TILE, :] = x_ref[:, cs]
        pad_ref[HALO + ROW_TILE:, :] = jnp.where(has_next, next_ref[:, cs], 0.0)
        acc = None
        for j in range(DN_CONV_K):
            win = pad_ref[pl.ds(HALO - half + j, ROW_TILE), :]
            term = win * cw_ref[j:j + 1, cs]
            acc = term if acc is None else acc + term
        y = _silu(acc)
        if hb < 2 * DN_QK_HEADS:
            y = y * lax.rsqrt(jnp.sum(y * y, axis=-1, keepdims=True) + NORM_EPS)
        qkv_ref[:, cs] = y.astype(BF16)
    ab = ab_ref[...]
    lane = lax.broadcasted_iota(jnp.int32, ab.shape, 1)
    beta = jax.nn.sigmoid(ab)
    g = -jnp.exp(alog_ref[...]) * jax.nn.softplus(ab + dtb_ref[...])
    ti = lax.broadcasted_iota(jnp.int32, (ROW_TILE, ROW_TILE), 0)
    tj = lax.broadcasted_iota(jnp.int32, (ROW_TILE, ROW_TILE), 1)
    prefix_mat = jnp.where(jnp.logical_and(ti // DN_CHUNK == tj // DN_CHUNK, tj <= ti), 1.0, 0.0)
    prefix = jnp.dot(prefix_mat, g, precision=lax.Precision.HIGHEST, preferred_element_type=F32)
    row_chunk = lax.broadcasted_iota(jnp.int32, ab.shape, 0) // DN_CHUNK
    total = jnp.zeros_like(prefix)
    for ch in range(ROW_TILE // DN_CHUNK):
        end = (ch + 1) * DN_CHUNK
        total = jnp.where(row_chunk == ch, prefix[end - 1:end, :], total)
    backward_lane = lane >= 3 * DN_V_HEADS
    gcum = jnp.where(backward_lane, total - prefix + g, prefix)
    gate_ref[...] = jnp.where(lane < 2 * DN_V_HEADS, beta, gcum)


def _dn_conv(qkv_pre, ab, conv_w, alog_vec, dtb_vec):
    b, lt, c = qkv_pre.shape
    nt = lt // ROW_TILE
    per = ROW_TILE // HALO
    nhalo = lt // HALO
    row = lambda w: pl.BlockSpec((None, ROW_TILE, w), lambda bi, r: (bi, r, 0))
    return pl.pallas_call(
        _dn_conv_body,
        grid=(b, nt),
        in_specs=[
            row(c),
            pl.BlockSpec((None, HALO, c), lambda bi, r: (bi, jnp.maximum(r * per - 1, 0), 0)),
            pl.BlockSpec((None, HALO, c), lambda bi, r: (bi, jnp.minimum((r + 1) * per, nhalo - 1), 0)),
            _resident(conv_w.shape), row(LANES), _resident((1, LANES)), _resident((1, LANES)),
        ],
        out_specs=[row(c), row(LANES)],
        out_shape=[jax.ShapeDtypeStruct((b, lt, c), BF16), jax.ShapeDtypeStruct((b, lt, LANES), F32)],
        scratch_shapes=[pltpu.VMEM((ROW_TILE + 2 * HALO, HEAD), F32)],
        compiler_params=_cparams("arbitrary", "arbitrary"),
        name="dn_conv_gates",
    )(qkv_pre, qkv_pre, qkv_pre, conv_w, ab, alog_vec, dtb_vec)


def _dot(a, b):
    return jnp.dot(a.astype(BF16), b.astype(BF16), preferred_element_type=F32)


def _dot_nt(a, b):
    return lax.dot_general(a.astype(BF16), b.astype(BF16), (((1,), (1,)), ((), ())),
                           preferred_element_type=F32)


def _dot_tn(a, b):
    return lax.dot_general(a.astype(BF16), b.astype(BF16), (((0,), (0,)), ((), ())),
                           preferred_element_type=F32)


def _split(a):
    hi = a.astype(BF16)
    return hi, (a - hi.astype(F32)).astype(BF16)


def _pair_diag(x):
    lane = lax.broadcasted_iota(jnp.int32, x.shape, 1)
    zero = jnp.zeros_like(x)
    c = x.shape[0]
    return jnp.concatenate([jnp.where(lane < c, x, zero), jnp.where(lane >= c, x, zero)], axis=0)


def _pair_matmul(a, x):
    a_hi, a_lo = _split(a)
    x_hi, x_lo = _split(x)
    lhs = jnp.concatenate([a_hi, a_lo, a_hi], axis=1)
    rhs = jnp.concatenate([_pair_diag(x_hi), _pair_diag(x_hi), _pair_diag(x_lo)], axis=0)
    return jnp.dot(lhs, rhs, preferred_element_type=F32)


def _delta_body(qf_ref, kf_ref, vf_ref, gf_ref, qb_ref, kb_ref, vb_ref, gb_ref,
                of_ref, ob_ref, s_ref):
    c = DN_CHUNK
    n_chunks = ROW_TILE // c
    rep = DN_V_HEADS // DN_QK_HEADS
    assert rep == 2 and 2 * c == LANES
    q_scale = HEAD ** -0.5

    @pl.when(pl.program_id(2) == 0)
    def _():
        s_ref[...] = jnp.zeros_like(s_ref)

    ci = lax.broadcasted_iota(jnp.int32, (c, LANES), 0)
    cl = lax.broadcasted_iota(jnp.int32, (c, LANES), 1)
    cj = cl % c
    head1 = cl >= c
    eye_p = jnp.where(ci == cj, 1.0, 0.0)

    walks = []
    for reverse, (q_ref, k_ref, v_ref, g_ref, o_ref) in enumerate(
            ((qf_ref, kf_ref, vf_ref, gf_ref, of_ref), (qb_ref, kb_ref, vb_ref, gb_ref, ob_ref))):
        gr = g_ref[...]
        gr8 = jnp.concatenate([gr, jnp.zeros((8 - gr.shape[0], ROW_TILE), F32)], axis=0)
        walks.append(dict(
            reverse=reverse, q_ref=q_ref, k_ref=k_ref, v_ref=v_ref, o_ref=o_ref,
            row=gr8, col=gr8.T,
            incl=(cj >= ci) if reverse else (cj <= ci),
            strict=(cj > ci) if reverse else (cj < ci),
            order=list(range(n_chunks - 1, -1, -1) if reverse else range(n_chunks))))

    items = []
    for w in walks:
        for ch in w["order"]:
            rows = slice(ch * c, (ch + 1) * c)
            kc = w["k_ref"][rows, :]
            qc = w["q_ref"][rows, :]
            gcol = [w["col"][rows, s:s + 1] for s in range(rep)]
            bcol = [w["col"][rows, rep + s:rep + s + 1] for s in range(rep)]
            last = 0 if w["reverse"] else c - 1
            gtot = [g[last:last + 1, :] for g in gcol]
            grow_p = jnp.concatenate([w["row"][s:s + 1, rows] for s in range(rep)], axis=1)
            gcol_p = jnp.where(head1, gcol[1], gcol[0])
            beta_p = jnp.where(head1, bcol[1], bcol[0])
            decay_p = jnp.exp(jnp.where(w["incl"], gcol_p - grow_p, NEG_BIG))
            k2 = jnp.concatenate([kc, kc], axis=0)
            a_p = jnp.where(w["strict"], beta_p * _dot_nt(kc, k2) * decay_p, 0.0)
            qkm_p = q_scale * _dot_nt(qc, k2) * decay_p
            items.append(dict(w=w, rows=rows, kf=kc.astype(F32), qf=qc.astype(F32), gcol=gcol,
                              bcol=bcol, gtot=gtot, a=a_p, qkm=qkm_p))

    for it in items:
        it["x"] = _pair_matmul(it["a"], it["a"])
        it["t"] = eye_p - it["a"]
    for _ in range(int(math.log2(c)) - 2):
        for it in items:
            r2 = _pair_matmul(jnp.concatenate([it["x"], it["t"]], axis=0), it["x"])
            it["x"] = r2[:c]
            it["t"] = it["t"] + r2[c:]
    for it in items:
        it["t"] = it["t"] + _pair_matmul(it["t"], it["x"])

    for it in items:
        w = it["w"]
        rhs = []
        for s in range(rep):
            eg = jnp.exp(it["gcol"][s])
            vf = w["v_ref"][it["rows"], s * HEAD:(s + 1) * HEAD].astype(F32)
            rhs.append(jnp.concatenate([vf * it["bcol"][s], it["kf"] * (it["bcol"][s] * eg)], axis=1))
        it["uw"] = _dot(_pair_diag(it["t"]), jnp.concatenate(rhs, axis=0))
    for it in items:
        it["qo"] = _dot(_pair_diag(it["qkm"]), it["uw"])
        it["gn"] = []
        for s in range(rep):
            k_dec = it["kf"] * jnp.exp(it["gtot"][s] - it["gcol"][s])
            it["gn"].append(_dot_tn(k_dec, it["uw"][s * c:(s + 1) * c]))

    for wi, w in enumerate(walks):
        state = [s_ref[wi, s] for s in range(rep)]
        for it in items:
            if it["w"] is not w:
                continue
            for s in range(rep):
                q_eff = it["qf"] * (q_scale * jnp.exp(it["gcol"][s])) - it["qo"][s * c:(s + 1) * c, HEAD:]
                rs = _dot(jnp.concatenate([it["gn"][s][:, HEAD:], q_eff], axis=0), state[s])
                state[s] = state[s] * jnp.exp(it["gtot"][s]) - rs[:HEAD] + it["gn"][s][:, :HEAD]
                w["o_ref"][it["rows"], s * HEAD:(s + 1) * HEAD] = rs[HEAD:] + it["qo"][s * c:(s + 1) * c, :HEAD]
        for s in range(rep):
            s_ref[wi, s] = state[s]


def _delta_rule(qkv, gate_rows):
    b, lt, _ = qkv.shape
    nt = lt // ROW_TILE
    rep = DN_V_HEADS // DN_QK_HEADS
    v_col0 = 2 * DN_KEY_DIM // (rep * HEAD)
    fwd = lambda r: r
    bwd = lambda r: jnp.where(r == 0, 0, nt - r)

    def specs(tile, d):
        return [
            pl.BlockSpec((None, ROW_TILE, HEAD), lambda bi, j, r: (bi, tile(r), j)),
            pl.BlockSpec((None, ROW_TILE, HEAD), lambda bi, j, r: (bi, tile(r), DN_QK_HEADS + j)),
            pl.BlockSpec((None, ROW_TILE, rep * HEAD), lambda bi, j, r: (bi, tile(r), v_col0 + j)),
            pl.BlockSpec((None, None, None, 2 * rep, ROW_TILE), lambda bi, j, r: (bi, d, j, 0, tile(r))),
        ]

    out = lambda tile: pl.BlockSpec((None, ROW_TILE, rep * HEAD), lambda bi, j, r: (bi, tile(r), j))
    o_shape = jax.ShapeDtypeStruct((b, lt, DN_VAL_DIM), F32)
    return pl.pallas_call(
        _delta_body,
        grid=(b, DN_QK_HEADS, nt),
        in_specs=specs(fwd, 0) + specs(bwd, 1),
        out_specs=[out(fwd), out(bwd)],
        out_shape=[o_shape, o_shape],
        scratch_shapes=[pltpu.VMEM((2, rep, HEAD, HEAD), F32)],
        compiler_params=_cparams("arbitrary", "arbitrary", "arbitrary"),
        name="dn_delta_rule",
    )(qkv, qkv, qkv, gate_rows, qkv, qkv, qkv, gate_rows)


def _outproj_body(*refs, n_o, head_norm, final_norm):
    o_refs = refs[:n_o]
    z_ref, h_ref, mod_ref, w_ref = refs[n_o:n_o + 4]
    rest = list(refs[n_o + 4:])
    og_ref = rest.pop(0) if head_norm else None
    fg_ref = rest.pop(0) if final_norm else None
    out_ref = rest.pop(0)
    d = h_ref.shape[-1]
    width = z_ref.shape[-1]
    parts = []
    for hb in range(width // HEAD):
        cs = slice(hb * HEAD, (hb + 1) * HEAD)
        o = o_refs[0][:, cs].astype(F32)
        for extra in o_refs[1:]:
            o = o + extra[:, cs].astype(F32)
        if head_norm:
            o = o * lax.rsqrt(jnp.mean(o * o, axis=-1, keepdims=True) + NORM_EPS) * og_ref[...]
        parts.append((o * _silu(z_ref[:, cs].astype(F32))).astype(BF16))
    y = jnp.dot(jnp.concatenate(parts, axis=1), w_ref[...], preferred_element_type=F32)
    hn = h_ref[...] + mod_ref[:, 2 * d:3 * d] * y
    if final_norm:
        hn = hn * lax.rsqrt(jnp.mean(hn * hn, axis=-1, keepdims=True) + NORM_EPS) * fg_ref[...]
    out_ref[...] = hn


def _outproj(o_list, z, h, mod4, w_out, layer, head_g=None, final_g=None):
    b, lt, d = h.shape
    nt = lt // ROW_TILE
    d3 = mod4.shape[-1]
    width = z.shape[-1]
    skip = 1 if final_g is not None else 0
    row = lambda c: pl.BlockSpec((None, ROW_TILE, c), lambda bi, r: (bi, r + skip, 0))
    mod_spec = pl.BlockSpec((None, None, 1, d3),
                            lambda bi, r: (layer, jnp.where(r + skip == 0, b, bi), 0, 0))
    in_specs = [row(width)] * len(o_list) + [row(width), row(d), mod_spec, _resident(w_out.shape)]
    args = list(o_list) + [z, h, mod4, w_out]
    if head_g is not None:
        in_specs.append(_resident((1, HEAD)))
        args.append(head_g)
    if final_g is not None:
        in_specs.append(_resident((1, d)))
        args.append(final_g)
    return pl.pallas_call(
        functools.partial(_outproj_body, n_o=len(o_list), head_norm=head_g is not None,
                          final_norm=final_g is not None),
        grid=(b, nt - skip),
        in_specs=in_specs,
        out_specs=pl.BlockSpec((None, ROW_TILE, d), lambda bi, r: (bi, r, 0)),
        out_shape=jax.ShapeDtypeStruct((b, lt - skip * ROW_TILE, d), F32),
        compiler_params=_cparams("arbitrary", "arbitrary"),
        name="mixer_outproj",
    )(*args)


def _rope_partner(x):
    lane = lax.broadcasted_iota(jnp.int32, x.shape, 1)
    quarter = HEAD // 4
    return jnp.where((lane % (2 * quarter)) < quarter,
                     pltpu.roll(x, HEAD - quarter, axis=1), pltpu.roll(x, quarter, axis=1))


def _att_inproj_body(h_ref, mod_ref, g_ref, w_ref, qg_ref, kg_ref, cos_ref, sin_ref,
                     q_ref, k_ref, v_ref, z_ref):
    d = h_ref.shape[-1]
    u = _modulated_norm(h_ref[...], g_ref[...], mod_ref[...], d).astype(BF16)
    cos = cos_ref[...]
    sin = sin_ref[...]
    q_scale = (HEAD ** -0.5) * math.log2(math.e)

    def normed_rope(x, g):
        xn = x * lax.rsqrt(jnp.mean(x * x, axis=-1, keepdims=True) + NORM_EPS) * g
        return xn * cos + _rope_partner(xn) * sin

    for hb in range(ATT_Q_HEADS):
        cs = slice(hb * HEAD, (hb + 1) * HEAD)
        x = jnp.dot(u, w_ref[:, cs], preferred_element_type=F32)
        q_ref[:, cs] = (normed_rope(x, qg_ref[...]) * q_scale).astype(BF16)
    for hb in range(ATT_KV_HEADS):
        cs = slice(hb * HEAD, (hb + 1) * HEAD)
        ws = slice(ATT_Q_DIM + hb * HEAD, ATT_Q_DIM + (hb + 1) * HEAD)
        x = jnp.dot(u, w_ref[:, ws], preferred_element_type=F32)
        k_ref[:, cs] = normed_rope(x, kg_ref[...]).astype(BF16)
    v0 = ATT_Q_DIM + ATT_KV_DIM
    v_ref[...] = jnp.dot(u, w_ref[:, v0:v0 + ATT_KV_DIM], preferred_element_type=F32).astype(BF16)
    z0 = v0 + ATT_KV_DIM
    for n in range(ATT_Q_DIM // N_COL_CHUNK):
        cs = slice(n * N_COL_CHUNK, (n + 1) * N_COL_CHUNK)
        ws = slice(z0 + n * N_COL_CHUNK, z0 + (n + 1) * N_COL_CHUNK)
        z_ref[:, cs] = jnp.dot(u, w_ref[:, ws], preferred_element_type=F32).astype(BF16)


def _att_inproj(h, mod4, norm_g, w_in, q_g, k_g, cos_t, sin_t, layer):
    b, lt, d = h.shape
    nt = lt // ROW_TILE
    d3 = mod4.shape[-1]
    row = lambda c: pl.BlockSpec((None, ROW_TILE, c), lambda bi, r: (bi, r, 0))
    tab = pl.BlockSpec((ROW_TILE, HEAD), lambda bi, r: (r, 0))
    return pl.pallas_call(
        _att_inproj_body,
        grid=(b, nt),
        in_specs=[row(d), _mod_spec(layer, b, d3), _resident((1, d)), _resident(w_in.shape),
                  _resident((1, HEAD)), _resident((1, HEAD)), tab, tab],
        out_specs=[row(ATT_Q_DIM), row(ATT_KV_DIM), row(ATT_KV_DIM), row(ATT_Q_DIM)],
        out_shape=[jax.ShapeDtypeStruct((b, lt, ATT_Q_DIM), BF16),
                   jax.ShapeDtypeStruct((b, lt, ATT_KV_DIM), BF16),
                   jax.ShapeDtypeStruct((b, lt, ATT_KV_DIM), BF16),
                   jax.ShapeDtypeStruct((b, lt, ATT_Q_DIM), BF16)],
        compiler_params=_cparams("arbitrary", "arbitrary"),
        name="att_inproj",
    )(h, mod4, norm_g, w_in, q_g, k_g, cos_t, sin_t)


def _attn_body(q_ref, k_ref, v_ref, o_ref, s_ref):
    r = pl.program_id(2)
    n_lat_chunks = k_ref.shape[0] // ROW_TILE - 1
    unroll = math.gcd(ATT_KV_UNROLL, n_lat_chunks)
    n_outer = jnp.where(r == 0, 0, n_lat_chunks // unroll)
    halves = ROW_TILE // LANES

    def score_chunk(g, m, chunk):
        off = pl.multiple_of(chunk * ROW_TILE, ROW_TILE)
        s = lax.dot_general(q_ref[:, g * HEAD:(g + 1) * HEAD], k_ref[pl.ds(off, ROW_TILE), :],
                            (((1,), (1,)), ((), ())), preferred_element_type=F32)
        s_ref[g % 2, chunk] = s
        for t in range(halves):
            m = jnp.maximum(m, s[:, t * LANES:(t + 1) * LANES])
        return m

    def value_chunk(g, m_row, l, acc, chunk):
        off = pl.multiple_of(chunk * ROW_TILE, ROW_TILE)
        p = jnp.exp2(s_ref[g % 2, chunk] - m_row)
        for t in range(halves):
            l = l + p[:, t * LANES:(t + 1) * LANES]
        vc = v_ref[pl.ds(off, ROW_TILE), :]
        return l, acc + jnp.dot(p.astype(BF16), vc, preferred_element_type=F32)

    def phase(g_score, g_value, m_row):
        def both(carry, chunk):
            m, l, acc = carry
            if g_score is not None:
                m = score_chunk(g_score, m, chunk)
            if g_value is not None:
                l, acc = value_chunk(g_value, m_row, l, acc, chunk)
            return m, l, acc

        def outer(i, carry):
            for u in range(unroll):
                carry = both(carry, 1 + i * unroll + u)
            return carry

        init = (jnp.full((ROW_TILE, LANES), NEG_BIG, F32), jnp.zeros((ROW_TILE, LANES), F32),
                jnp.zeros((ROW_TILE, HEAD), F32))
        m, l, acc = lax.fori_loop(0, n_outer, outer, both(init, 0))
        if g_value is not None:
            o_ref[:, g_value * HEAD:(g_value + 1) * HEAD] = (
                acc / jnp.sum(l, axis=-1, keepdims=True)).astype(BF16)
        return jnp.max(m, axis=-1, keepdims=True) if g_score is not None else None

    m_row = phase(0, None, None)
    for g in range(ATT_GROUP):
        m_row = phase(g + 1 if g + 1 < ATT_GROUP else None, g, m_row)


def _attention(q, k, v):
    b, lt, _ = q.shape
    nt = lt // ROW_TILE
    gw = ATT_GROUP * HEAD
    return pl.pallas_call(
        _attn_body,
        grid=(b, ATT_KV_HEADS, nt),
        in_specs=[
            pl.BlockSpec((None, ROW_TILE, gw), lambda bi, hk, r: (bi, r, hk)),
            pl.BlockSpec((None, lt, HEAD), lambda bi, hk, r: (bi, 0, hk)),
            pl.BlockSpec((None, lt, HEAD), lambda bi, hk, r: (bi, 0, hk)),
        ],
        out_specs=pl.BlockSpec((None, ROW_TILE, gw), lambda bi, hk, r: (bi, r, hk)),
        out_shape=jax.ShapeDtypeStruct((b, lt, ATT_Q_DIM), BF16),
        scratch_shapes=[pltpu.VMEM((2, nt, ROW_TILE, ROW_TILE), F32)],
        compiler_params=_cparams("arbitrary", "arbitrary", "arbitrary"),
        name="gqa_attention",
    )(q, k, v)


def _rope_tables(n_lat, n_ctx):
    t = jnp.arange(n_lat)
    row = (t // GRID_W).astype(F32)
    col = (t % GRID_W).astype(F32)
    axis_dim = HEAD // 2
    inv = ROPE_THETA ** (-jnp.arange(0, axis_dim, 2, dtype=F32) / axis_dim)
    ang_r = row[:, None] * inv
    ang_c = col[:, None] * inv
    cos = jnp.concatenate([jnp.cos(ang_r)] * 2 + [jnp.cos(ang_c)] * 2, axis=-1)
    sin = jnp.concatenate([-jnp.sin(ang_r), jnp.sin(ang_r), -jnp.sin(ang_c), jnp.sin(ang_c)], axis=-1)
    cos = jnp.concatenate([jnp.ones((n_ctx, HEAD), F32), cos], axis=0)
    sin = jnp.concatenate([jnp.zeros((n_ctx, HEAD), F32), sin], axis=0)
    return cos, sin


def _lane_pad(vec, offset):
    out = jnp.zeros((1, LANES), F32)
    return lax.dynamic_update_slice(out, vec.reshape(1, -1).astype(F32), (0, offset))


def kernel(x, c, ctx, c_ctx, norm_g, ada_w, ada_b, dn_w_in, dn_conv_w, dn_a_log, dn_dt_bias,
           dn_o_norm_g, dn_w_out, att_w_in, att_q_norm_g, att_k_norm_g, att_w_out, final_norm_g):
    b, n_lat, d = x.shape
    n_ctx = ctx.shape[1]
    depth = norm_g.shape[0]
    assert n_ctx == ROW_TILE and n_lat % ROW_TILE == 0 and b < MOD_ROWS and d % LANES == 0
    lt = n_ctx + n_lat
    rep = DN_V_HEADS // DN_QK_HEADS

    cvec = jnp.concatenate([c, c_ctx[None, :], jnp.zeros((MOD_ROWS - b - 1, d), F32)], axis=0)
    mod4 = _modulation(cvec, ada_w, ada_b).reshape(depth, MOD_ROWS, 1, 3 * d)
    h = jnp.concatenate([ctx, x], axis=1)
    cos_t, sin_t = _rope_tables(n_lat, n_ctx)

    for i in range(depth):
        j = i // 2
        last = i == depth - 1
        g_i = norm_g[i].reshape(1, d)
        if i % 2 == 0:
            w_in = dn_w_in[j].astype(BF16)
            w_main = w_in[:, :DN_CONV_DIM + DN_VAL_DIM]
            w_ab = jnp.pad(w_in[:, DN_CONV_DIM + DN_VAL_DIM:], ((0, 0), (0, LANES - 4 * DN_V_HEADS)))
            qkv_pre, z, ab = _dn_inproj(h, mod4, g_i, w_main, w_ab, i)
            alog_vec = _lane_pad(dn_a_log[j], 2 * DN_V_HEADS)
            dtb_vec = _lane_pad(dn_dt_bias[j], 2 * DN_V_HEADS)
            qkv, gates = _dn_conv(qkv_pre, ab, dn_conv_w[j], alog_vec, dtb_vec)
            gt = gates[:, :, :4 * DN_V_HEADS].reshape(b, lt, 2, 2, DN_QK_HEADS, rep)
            gate_rows = jnp.transpose(gt[:, :, ::-1], (0, 3, 4, 2, 5, 1)).reshape(
                b, 2, DN_QK_HEADS, 2 * rep, lt)
            o_f, o_b = _delta_rule(qkv, gate_rows)
            h = _outproj([o_f, o_b], z, h, mod4, dn_w_out[j].astype(BF16), i,
                         head_g=dn_o_norm_g[j].reshape(1, HEAD),
                         final_g=final_norm_g.reshape(1, d) if last else None)
        else:
            q, k, v, z = _att_inproj(h, mod4, g_i, att_w_in[j].astype(BF16),
                                     att_q_norm_g[j].reshape(1, HEAD), att_k_norm_g[j].reshape(1, HEAD),
                                     cos_t, sin_t, i)
            o = _attention(q, k, v)
            h = _outproj([o], z, h, mod4, att_w_out[j].astype(BF16), i,
                         final_g=final_norm_g.reshape(1, d) if last else None)
    return h
```

```python
import functools
import math

import jax
import jax.numpy as jnp
from jax import lax
from jax.experimental import pallas as pl
from jax.experimental.pallas import tpu as pltpu

F32 = jnp.float32
BF16 = jnp.bfloat16

NORM_EPS = 1e-6
GRID_W = 64
ROPE_THETA = 10000.0

HEAD = 128
DN_QK_HEADS = 8
DN_V_HEADS = 16
DN_KEY_DIM = DN_QK_HEADS * HEAD
DN_VAL_DIM = DN_V_HEADS * HEAD
DN_CONV_DIM = 2 * DN_KEY_DIM + DN_VAL_DIM
DN_CONV_K = 5
DN_CHUNK = 64
ATT_Q_HEADS = 8
ATT_KV_HEADS = 2
ATT_GROUP = ATT_Q_HEADS // ATT_KV_HEADS
ATT_Q_DIM = ATT_Q_HEADS * HEAD
ATT_KV_DIM = ATT_KV_HEADS * HEAD

ROW_TILE = 256
LANES = 128
HALO = 8
MOD_ROWS = 16
N_COL_CHUNK = 512
DN_RANK_GROUP = 4
VMEM_LIMIT = 56 * 1024 * 1024
NEG_BIG = -1e30


def _cparams(*sem):
    return pltpu.CompilerParams(dimension_semantics=sem, vmem_limit_bytes=VMEM_LIMIT)


def _resident(shape):
    return pl.BlockSpec(shape, lambda *_: (0,) * len(shape), pipeline_mode=pl.Buffered(1))


def _silu(x):
    return x * jax.nn.sigmoid(x)


def _mod_body(c_ref, w_ref, b_ref, o_ref):
    sc = _silu(c_ref[...])
    o_ref[...] = jnp.dot(sc, w_ref[...], preferred_element_type=F32) + b_ref[...]


def _modulation(cvec, ada_w, ada_b):
    depth, d, d3 = ada_w.shape
    tn = 1024
    return pl.pallas_call(
        _mod_body,
        grid=(depth, d3 // tn),
        in_specs=[
            pl.BlockSpec((MOD_ROWS, d), lambda i, j: (0, 0)),
            pl.BlockSpec((None, d, tn), lambda i, j: (i, 0, j)),
            pl.BlockSpec((None, 1, tn), lambda i, j: (i, 0, j)),
        ],
        out_specs=pl.BlockSpec((None, MOD_ROWS, tn), lambda i, j: (i, 0, j)),
        out_shape=jax.ShapeDtypeStruct((depth, MOD_ROWS, d3), F32),
        compiler_params=_cparams("arbitrary", "arbitrary"),
        name="adaln_modulation",
    )(cvec, ada_w, ada_b.reshape(depth, 1, d3))


def _mod_spec(layer, batch, d3):
    return pl.BlockSpec((None, None, 1, d3),
                        lambda b, r: (layer, jnp.where(r == 0, batch, b), 0, 0))


def _modulated_norm(x, g, mod, d):
    ms = jnp.mean(x * x, axis=-1, keepdims=True)
    xn = x * lax.rsqrt(ms + NORM_EPS) * g
    return xn * (1.0 + mod[:, d:2 * d]) + mod[:, 0:d]


def _dn_inproj_body(h_ref, mod_ref, g_ref, w_ref, wab_ref, qkv_ref, z_ref, ab_ref):
    d = h_ref.shape[-1]
    u = _modulated_norm(h_ref[...], g_ref[...], mod_ref[...], d).astype(BF16)
    for n in range(DN_CONV_DIM // N_COL_CHUNK):
        cs = slice(n * N_COL_CHUNK, (n + 1) * N_COL_CHUNK)
        qkv_ref[:, cs] = jnp.dot(u, w_ref[:, cs], preferred_element_type=F32)
    for n in range(DN_VAL_DIM // N_COL_CHUNK):
        cs = slice(n * N_COL_CHUNK, (n + 1) * N_COL_CHUNK)
        ws = slice(DN_CONV_DIM + n * N_COL_CHUNK, DN_CONV_DIM + (n + 1) * N_COL_CHUNK)
        z_ref[:, cs] = jnp.dot(u, w_ref[:, ws], preferred_element_type=F32).astype(BF16)
    ab_ref[...] = jnp.dot(u, wab_ref[...], preferred_element_type=F32)


def _dn_inproj(h, mod4, norm_g, w_main, w_ab, layer):
    b, lt, d = h.shape
    nt = lt // ROW_TILE
    d3 = mod4.shape[-1]
    row = lambda c: pl.BlockSpec((None, ROW_TILE, c), lambda bi, r: (bi, r, 0))
    return pl.pallas_call(
        _dn_inproj_body,
        grid=(b, nt),
        in_specs=[row(d), _mod_spec(layer, b, d3), _resident((1, d)),
                  _resident(w_main.shape), _resident(w_ab.shape)],
        out_specs=[row(DN_CONV_DIM), row(DN_VAL_DIM), row(LANES)],
        out_shape=[jax.ShapeDtypeStruct((b, lt, DN_CONV_DIM), F32),
                   jax.ShapeDtypeStruct((b, lt, DN_VAL_DIM), BF16),
                   jax.ShapeDtypeStruct((b, lt, LANES), F32)],
        compiler_params=_cparams("arbitrary", "arbitrary"),
        name="dn_inproj",
    )(h, mod4, norm_g, w_main, w_ab)


def _dn_conv_body(x_ref, prev_ref, next_ref, cw_ref, ab_ref, alog_ref, dtb_ref,
                  qkv_ref, gate_ref, pad_ref):
    r = pl.program_id(1)
    nt = pl.num_programs(1)
    has_prev = r > 1
    has_next = jnp.logical_and(r > 0, r < nt - 1)
    half = DN_CONV_K // 2
    for hb in range(DN_CONV_DIM // HEAD):
        cs = slice(hb * HEAD, (hb + 1) * HEAD)
        pad_ref[0:HALO, :] = jnp.where(has_prev, prev_ref[:, cs], 0.0)
        pad_ref[HALO:HALO + ROW_TILE, :] = x_ref[:, cs]
        pad_ref[HALO + ROW_TILE:, :] = jnp.where(has_next, next_ref[:, cs], 0.0)
        acc = None
        for j in range(DN_CONV_K):
            win = pad_ref[pl.ds(HALO - half + j, ROW_TILE), :]
            term = win * cw_ref[j:j + 1, cs]
            acc = term if acc is None else acc + term
        y = _silu(acc)
        if hb < 2 * DN_QK_HEADS:
            y = y * lax.rsqrt(jnp.sum(y * y, axis=-1, keepdims=True) + NORM_EPS)
        qkv_ref[:, cs] = y.astype(BF16)
    ab = ab_ref[...]
    lane = lax.broadcasted_iota(jnp.int32, ab.shape, 1)
    beta = jax.nn.sigmoid(ab)
    g = -jnp.exp(alog_ref[...]) * jax.nn.softplus(ab + dtb_ref[...])
    ti = lax.broadcasted_iota(jnp.int32, (ROW_TILE, ROW_TILE), 0)
    tj = lax.broadcasted_iota(jnp.int32, (ROW_TILE, ROW_TILE), 1)
    prefix_mat = jnp.where(jnp.logical_and(ti // DN_CHUNK == tj // DN_CHUNK, tj <= ti), 1.0, 0.0)
    prefix = jnp.dot(prefix_mat, g, precision=lax.Precision.HIGHEST, preferred_element_type=F32)
    row_chunk = lax.broadcasted_iota(jnp.int32, ab.shape, 0) // DN_CHUNK
    total = jnp.zeros_like(prefix)
    for ch in range(ROW_TILE // DN_CHUNK):
        end = (ch + 1) * DN_CHUNK
        total = jnp.where(row_chunk == ch, prefix[end - 1:end, :], total)
    backward_lane = lane >= 3 * DN_V_HEADS
    gcum = jnp.where(backward_lane, total - prefix + g, prefix)
    gate_ref[...] = jnp.where(lane < 2 * DN_V_HEADS, beta, gcum)


def _dn_conv(qkv_pre, ab, conv_w, alog_vec, dtb_vec):
    b, lt, c = qkv_pre.shape
    nt = lt // ROW_TILE
    per = ROW_TILE // HALO
    nhalo = lt // HALO
    row = lambda w: pl.BlockSpec((None, ROW_TILE, w), lambda bi, r: (bi, r, 0))
    return pl.pallas_call(
        _dn_conv_body,
        grid=(b, nt),
        in_specs=[
            row(c),
            pl.BlockSpec((None, HALO, c), lambda bi, r: (bi, jnp.maximum(r * per - 1, 0), 0)),
            pl.BlockSpec((None, HALO, c), lambda bi, r: (bi, jnp.minimum((r + 1) * per, nhalo - 1), 0)),
            _resident(conv_w.shape), row(LANES), _resident((1, LANES)), _resident((1, LANES)),
        ],
        out_specs=[row(c), row(LANES)],
        out_shape=[jax.ShapeDtypeStruct((b, lt, c), BF16), jax.ShapeDtypeStruct((b, lt, LANES), F32)],
        scratch_shapes=[pltpu.VMEM((ROW_TILE + 2 * HALO, HEAD), F32)],
        compiler_params=_cparams("arbitrary", "arbitrary"),
        name="dn_conv_gates",
    )(qkv_pre, qkv_pre, qkv_pre, conv_w, ab, alog_vec, dtb_vec)


def _dot(a, b):
    return jnp.dot(a.astype(BF16), b.astype(BF16), preferred_element_type=F32)


def _dot_nt(a, b):
    return lax.dot_general(a.astype(BF16), b.astype(BF16), (((1,), (1,)), ((), ())),
                           preferred_element_type=F32)


def _dot_tn(a, b):
    return lax.dot_general(a.astype(BF16), b.astype(BF16), (((0,), (0,)), ((), ())),
                           preferred_element_type=F32)


def _split(a):
    hi = a.astype(BF16)
    return hi, (a - hi.astype(F32)).astype(BF16)


def _pair_diag(x):
    lane = lax.broadcasted_iota(jnp.int32, x.shape, 1)
    zero = jnp.zeros_like(x)
    c = x.shape[0]
    return jnp.concatenate([jnp.where(lane < c, x, zero), jnp.where(lane >= c, x, zero)], axis=0)


def _pair_matmul(a, x):
    a_hi, a_lo = _split(a)
    x_hi, x_lo = _split(x)
    lhs = jnp.concatenate([a_hi, a_lo, a_hi], axis=1)
    rhs = jnp.concatenate([_pair_diag(x_hi), _pair_diag(x_hi), _pair_diag(x_lo)], axis=0)
    return jnp.dot(lhs, rhs, preferred_element_type=F32)


def _delta_body(qf_ref, kf_ref, vf_ref, gf_ref, qb_ref, kb_ref, vb_ref, gb_ref,
                of_ref, ob_ref, s_ref):
    c = DN_CHUNK
    n_chunks = ROW_TILE // c
    rep = DN_V_HEADS // DN_QK_HEADS
    assert rep == 2 and 2 * c == LANES
    q_scale = HEAD ** -0.5

    @pl.when(pl.program_id(2) == 0)
    def _():
        s_ref[...] = jnp.zeros_like(s_ref)

    ci = lax.broadcasted_iota(jnp.int32, (c, LANES), 0)
    cl = lax.broadcasted_iota(jnp.int32, (c, LANES), 1)
    cj = cl % c
    head1 = cl >= c
    eye_p = jnp.where(ci == cj, 1.0, 0.0)

    walks = []
    for reverse, (q_ref, k_ref, v_ref, g_ref, o_ref) in enumerate(
            ((qf_ref, kf_ref, vf_ref, gf_ref, of_ref), (qb_ref, kb_ref, vb_ref, gb_ref, ob_ref))):
        gr = g_ref[...]
        gr8 = jnp.concatenate([gr, jnp.zeros((8 - gr.shape[0], ROW_TILE), F32)], axis=0)
        walks.append(dict(
            reverse=reverse, q_ref=q_ref, k_ref=k_ref, v_ref=v_ref, o_ref=o_ref,
            row=gr8, col=gr8.T,
            incl=(cj >= ci) if reverse else (cj <= ci),
            strict=(cj > ci) if reverse else (cj < ci),
            order=list(range(n_chunks - 1, -1, -1) if reverse else range(n_chunks))))

    def prepare(w, ch):
        rows = slice(ch * c, (ch + 1) * c)
        kc = w["k_ref"][rows, :]
        qc = w["q_ref"][rows, :]
        gcol = [w["col"][rows, s:s + 1] for s in range(rep)]
        bcol = [w["col"][rows, rep + s:rep + s + 1] for s in range(rep)]
        last = 0 if w["reverse"] else c - 1
        gtot = [g[last:last + 1, :] for g in gcol]
        grow_p = jnp.concatenate([w["row"][s:s + 1, rows] for s in range(rep)], axis=1)
        gcol_p = jnp.where(head1, gcol[1], gcol[0])
        beta_p = jnp.where(head1, bcol[1], bcol[0])
        decay_p = jnp.exp(jnp.where(w["incl"], gcol_p - grow_p, NEG_BIG))
        k2 = jnp.concatenate([kc, kc], axis=0)
        a_p = jnp.where(w["strict"], beta_p * _dot_nt(kc, k2) * decay_p, 0.0)
        qkm_p = q_scale * _dot_nt(qc, k2) * decay_p
        return dict(w=w, rows=rows, kf=kc.astype(F32), qf=qc.astype(F32), gcol=gcol,
                    bcol=bcol, gtot=gtot, a=a_p, qkm=qkm_p)

    def invert(items, between_levels=()):
        fill = list(between_levels)
        for it in items:
            it["x"] = _pair_matmul(it["a"], it["a"])
            it["t"] = eye_p - it["a"]
        for _ in range(int(math.log2(c)) - 2):
            if fill:
                fill.pop(0)()
            for it in items:
                r2 = _pair_matmul(jnp.concatenate([it["x"], it["t"]], axis=0), it["x"])
                it["x"] = r2[:c]
                it["t"] = it["t"] + r2[c:]
        for thunk in fill:
            thunk()
        for it in items:
            it["t"] = it["t"] + _pair_matmul(it["t"], it["x"])

    def chunk_products(items):
        for it in items:
            rhs = []
            for s in range(rep):
                eg = jnp.exp(it["gcol"][s])
                vf = it["w"]["v_ref"][it["rows"], s * HEAD:(s + 1) * HEAD].astype(F32)
                rhs.append(jnp.concatenate([vf * it["bcol"][s], it["kf"] * (it["bcol"][s] * eg)], axis=1))
            it["uw"] = _dot(_pair_diag(it["t"]), jnp.concatenate(rhs, axis=0))
        for it in items:
            it["qo"] = _dot(_pair_diag(it["qkm"]), it["uw"])
            it["gn"] = []
            for s in range(rep):
                k_dec = it["kf"] * jnp.exp(it["gtot"][s] - it["gcol"][s])
                it["gn"].append(_dot_tn(k_dec, it["uw"][s * c:(s + 1) * c]))

    def advance(items):
        for it in items:
            state = it["w"]["state"]
            for s in range(rep):
                q_eff = it["qf"] * (q_scale * jnp.exp(it["gcol"][s])) - it["qo"][s * c:(s + 1) * c, HEAD:]
                rs = _dot(jnp.concatenate([it["gn"][s][:, HEAD:], q_eff], axis=0), state[s])
                state[s] = state[s] * jnp.exp(it["gtot"][s]) - rs[:HEAD] + it["gn"][s][:, :HEAD]
                it["w"]["o_ref"][it["rows"], s * HEAD:(s + 1) * HEAD] = (
                    rs[HEAD:] + it["qo"][s * c:(s + 1) * c, :HEAD]).astype(BF16)

    for wi, w in enumerate(walks):
        w["state"] = [s_ref[wi, s] for s in range(rep)]
    groups = [[prepare(w, w["order"][rank]) for rank in range(first, first + DN_RANK_GROUP) for w in walks]
              for first in range(0, n_chunks, DN_RANK_GROUP)]
    pending = []
    for items in groups:
        invert(items, pending)
        chunk_products(items)
        pending = [functools.partial(advance, items[i:i + len(walks)])
                   for i in range(0, len(items), len(walks))]
    for thunk in pending:
        thunk()
    for wi, w in enumerate(walks):
        for s in range(rep):
            s_ref[wi, s] = w["state"][s]


def _delta_rule(qkv, gate_rows):
    b, lt, _ = qkv.shape
    nt = lt // ROW_TILE
    rep = DN_V_HEADS // DN_QK_HEADS
    v_col0 = 2 * DN_KEY_DIM // (rep * HEAD)
    fwd = lambda r: r
    bwd = lambda r: jnp.where(r == 0, 0, nt - r)

    def specs(tile, d):
        return [
            pl.BlockSpec((None, ROW_TILE, HEAD), lambda bi, j, r: (bi, tile(r), j)),
            pl.BlockSpec((None, ROW_TILE, HEAD), lambda bi, j, r: (bi, tile(r), DN_QK_HEADS + j)),
            pl.BlockSpec((None, ROW_TILE, rep * HEAD), lambda bi, j, r: (bi, tile(r), v_col0 + j)),
            pl.BlockSpec((None, None, None, 2 * rep, ROW_TILE), lambda bi, j, r: (bi, d, j, 0, tile(r))),
        ]

    out = lambda tile: pl.BlockSpec((None, ROW_TILE, rep * HEAD), lambda bi, j, r: (bi, tile(r), j))
    o_shape = jax.ShapeDtypeStruct((b, lt, DN_VAL_DIM), BF16)
    return pl.pallas_call(
        _delta_body,
        grid=(b, DN_QK_HEADS, nt),
        in_specs=specs(fwd, 0) + specs(bwd, 1),
        out_specs=[out(fwd), out(bwd)],
        out_shape=[o_shape, o_shape],
        scratch_shapes=[pltpu.VMEM((2, rep, HEAD, HEAD), F32)],
        compiler_params=_cparams("arbitrary", "arbitrary", "arbitrary"),
        name="dn_delta_rule",
    )(qkv, qkv, qkv, gate_rows, qkv, qkv, qkv, gate_rows)


def _outproj_body(*refs, n_o, head_norm, final_norm):
    o_refs = refs[:n_o]
    z_ref, h_ref, mod_ref, w_ref = refs[n_o:n_o + 4]
    rest = list(refs[n_o + 4:])
    og_ref = rest.pop(0) if head_norm else None
    fg_ref = rest.pop(0) if final_norm else None
    out_ref = rest.pop(0)
    d = h_ref.shape[-1]
    width = z_ref.shape[-1]
    parts = []
    for hb in range(width // HEAD):
        cs = slice(hb * HEAD, (hb + 1) * HEAD)
        o = o_refs[0][:, cs].astype(F32)
        for extra in o_refs[1:]:
            o = o + extra[:, cs].astype(F32)
        if head_norm:
            o = o * lax.rsqrt(jnp.mean(o * o, axis=-1, keepdims=True) + NORM_EPS) * og_ref[...]
        parts.append((o * _silu(z_ref[:, cs].astype(F32))).astype(BF16))
    y = jnp.dot(jnp.concatenate(parts, axis=1), w_ref[...], preferred_element_type=F32)
    hn = h_ref[...] + mod_ref[:, 2 * d:3 * d] * y
    if final_norm:
        hn = hn * lax.rsqrt(jnp.mean(hn * hn, axis=-1, keepdims=True) + NORM_EPS) * fg_ref[...]
    out_ref[...] = hn


def _outproj(o_list, z, h, mod4, w_out, layer, head_g=None, final_g=None):
    b, lt, d = h.shape
    nt = lt // ROW_TILE
    d3 = mod4.shape[-1]
    width = z.shape[-1]
    skip = 1 if final_g is not None else 0
    row = lambda c: pl.BlockSpec((None, ROW_TILE, c), lambda bi, r: (bi, r + skip, 0))
    mod_spec = pl.BlockSpec((None, None, 1, d3),
                            lambda bi, r: (layer, jnp.where(r + skip == 0, b, bi), 0, 0))
    in_specs = [row(width)] * len(o_list) + [row(width), row(d), mod_spec, _resident(w_out.shape)]
    args = list(o_list) + [z, h, mod4, w_out]
    if head_g is not None:
        in_specs.append(_resident((1, HEAD)))
        args.append(head_g)
    if final_g is not None:
        in_specs.append(_resident((1, d)))
        args.append(final_g)
    return pl.pallas_call(
        functools.partial(_outproj_body, n_o=len(o_list), head_norm=head_g is not None,
                          final_norm=final_g is not None),
        grid=(b, nt - skip),
        in_specs=in_specs,
        out_specs=pl.BlockSpec((None, ROW_TILE, d), lambda bi, r: (bi, r, 0)),
        out_shape=jax.ShapeDtypeStruct((b, lt - skip * ROW_TILE, d), F32),
        compiler_params=_cparams("arbitrary", "arbitrary"),
        name="mixer_outproj",
    )(*args)


def _rope_partner(x):
    lane = lax.broadcasted_iota(jnp.int32, x.shape, 1)
    quarter = HEAD // 4
    return jnp.where((lane % (2 * quarter)) < quarter,
                     pltpu.roll(x, HEAD - quarter, axis=1), pltpu.roll(x, quarter, axis=1))


def _att_inproj_body(h_ref, mod_ref, g_ref, w_ref, wvt_ref, qg_ref, kg_ref, cos_ref, sin_ref,
                     q_ref, k_ref, vt_ref, z_ref):
    d = h_ref.shape[-1]
    u = _modulated_norm(h_ref[...], g_ref[...], mod_ref[...], d).astype(BF16)
    cos = cos_ref[...]
    sin = sin_ref[...]
    q_scale = (HEAD ** -0.5) * math.log2(math.e)

    def normed_rope(x, g):
        xn = x * lax.rsqrt(jnp.mean(x * x, axis=-1, keepdims=True) + NORM_EPS) * g
        return xn * cos + _rope_partner(xn) * sin

    pair = 2 * HEAD
    for pb in range(ATT_Q_DIM // pair):
        x2 = jnp.dot(u, w_ref[:, pb * pair:(pb + 1) * pair], preferred_element_type=F32)
        for t in range(2):
            cs = slice(pb * pair + t * HEAD, pb * pair + (t + 1) * HEAD)
            q_ref[:, cs] = (normed_rope(x2[:, t * HEAD:(t + 1) * HEAD], qg_ref[...]) * q_scale).astype(BF16)
    for pb in range(ATT_KV_DIM // pair):
        x2 = jnp.dot(u, w_ref[:, ATT_Q_DIM + pb * pair:ATT_Q_DIM + (pb + 1) * pair],
                     preferred_element_type=F32)
        for t in range(2):
            cs = slice(pb * pair + t * HEAD, pb * pair + (t + 1) * HEAD)
            k_ref[:, cs] = normed_rope(x2[:, t * HEAD:(t + 1) * HEAD], kg_ref[...]).astype(BF16)
    vt = lax.dot_general(wvt_ref[...], u, (((1,), (1,)), ((), ())), preferred_element_type=F32)
    for hk in range(ATT_KV_HEADS):
        vt_ref[hk] = vt[hk * HEAD:(hk + 1) * HEAD, :].astype(BF16)
    z0 = ATT_Q_DIM + 2 * ATT_KV_DIM
    for n in range(ATT_Q_DIM // N_COL_CHUNK):
        cs = slice(n * N_COL_CHUNK, (n + 1) * N_COL_CHUNK)
        ws = slice(z0 + n * N_COL_CHUNK, z0 + (n + 1) * N_COL_CHUNK)
        z_ref[:, cs] = jnp.dot(u, w_ref[:, ws], preferred_element_type=F32).astype(BF16)


def _att_inproj(h, mod4, norm_g, w_in, w_vt, q_g, k_g, cos_t, sin_t, layer):
    b, lt, d = h.shape
    nt = lt // ROW_TILE
    d3 = mod4.shape[-1]
    row = lambda c: pl.BlockSpec((None, ROW_TILE, c), lambda bi, r: (bi, r, 0))
    tab = pl.BlockSpec((ROW_TILE, HEAD), lambda bi, r: (r, 0))
    return pl.pallas_call(
        _att_inproj_body,
        grid=(b, nt),
        in_specs=[row(d), _mod_spec(layer, b, d3), _resident((1, d)), _resident(w_in.shape),
                  _resident(w_vt.shape), _resident((1, HEAD)), _resident((1, HEAD)), tab, tab],
        out_specs=[row(ATT_Q_DIM), row(ATT_KV_DIM),
                   pl.BlockSpec((None, ATT_KV_HEADS, None, HEAD, ROW_TILE), lambda bi, r: (bi, 0, r, 0, 0)),
                   row(ATT_Q_DIM)],
        out_shape=[jax.ShapeDtypeStruct((b, lt, ATT_Q_DIM), BF16),
                   jax.ShapeDtypeStruct((b, lt, ATT_KV_DIM), BF16),
                   jax.ShapeDtypeStruct((b, ATT_KV_HEADS, nt, HEAD, ROW_TILE), BF16),
                   jax.ShapeDtypeStruct((b, lt, ATT_Q_DIM), BF16)],
        compiler_params=_cparams("arbitrary", "arbitrary"),
        name="att_inproj",
    )(h, mod4, norm_g, w_in, w_vt, q_g, k_g, cos_t, sin_t)


def _attn_body(q_ref, k_ref, vt_ref, o_ref, s_ref):
    sub = 8
    fold = lambda x, op: op(x.reshape(ROW_TILE // sub, sub, ROW_TILE), axis=0)

    def score_chunk(g, m, chunk):
        st = lax.dot_general(k_ref[chunk * ROW_TILE:(chunk + 1) * ROW_TILE, :],
                             q_ref[:, g * HEAD:(g + 1) * HEAD],
                             (((1,), (1,)), ((), ())), preferred_element_type=F32)
        s_ref[g % 2, chunk] = st
        return jnp.maximum(m, fold(st, jnp.max))

    def value_chunk(g, m_row, l, acc, chunk):
        pt = jnp.exp2(s_ref[g % 2, chunk] - m_row)
        l = l + fold(pt, jnp.sum)
        return l, acc + jnp.dot(vt_ref[chunk], pt.astype(BF16), preferred_element_type=F32)

    def phase(n_chunks, g_score, g_value, m_row):
        m = jnp.full((sub, ROW_TILE), NEG_BIG, F32)
        l = jnp.zeros((sub, ROW_TILE), F32)
        acc = jnp.zeros((HEAD, ROW_TILE), F32)
        for chunk in range(n_chunks):
            if g_score is not None:
                m = score_chunk(g_score, m, chunk)
            if g_value is not None:
                l, acc = value_chunk(g_value, m_row, l, acc, chunk)
        if g_value is not None:
            o_t = acc / jnp.sum(l, axis=0, keepdims=True)
            o_ref[:, g_value * HEAD:(g_value + 1) * HEAD] = o_t.T.astype(BF16)
        return jnp.max(m, axis=0, keepdims=True) if g_score is not None else None

    def attend(n_chunks):
        m_row = phase(n_chunks, 0, None, None)
        for g in range(ATT_GROUP):
            m_row = phase(n_chunks, g + 1 if g + 1 < ATT_GROUP else None, g, m_row)

    is_ctx = pl.program_id(2) == 0
    pl.when(is_ctx)(functools.partial(attend, 1))
    pl.when(jnp.logical_not(is_ctx))(functools.partial(attend, k_ref.shape[0] // ROW_TILE))


def _attention(q, k, vt):
    b, lt, _ = q.shape
    nt = lt // ROW_TILE
    gw = ATT_GROUP * HEAD
    return pl.pallas_call(
        _attn_body,
        grid=(b, ATT_KV_HEADS, nt),
        in_specs=[
            pl.BlockSpec((None, ROW_TILE, gw), lambda bi, hk, r: (bi, r, hk)),
            pl.BlockSpec((None, lt, HEAD), lambda bi, hk, r: (bi, 0, hk)),
            pl.BlockSpec((None, None, nt, HEAD, ROW_TILE), lambda bi, hk, r: (bi, hk, 0, 0, 0)),
        ],
        out_specs=pl.BlockSpec((None, ROW_TILE, gw), lambda bi, hk, r: (bi, r, hk)),
        out_shape=jax.ShapeDtypeStruct((b, lt, ATT_Q_DIM), BF16),
        scratch_shapes=[pltpu.VMEM((2, nt, ROW_TILE, ROW_TILE), F32)],
        compiler_params=_cparams("arbitrary", "arbitrary", "arbitrary"),
        name="gqa_attention",
    )(q, k, vt)


def _rope_tables(n_lat, n_ctx):
    t = jnp.arange(n_lat)
    row = (t // GRID_W).astype(F32)
    col = (t % GRID_W).astype(F32)
    axis_dim = HEAD // 2
    inv = ROPE_THETA ** (-jnp.arange(0, axis_dim, 2, dtype=F32) / axis_dim)
    ang_r = row[:, None] * inv
    ang_c = col[:, None] * inv
    cos = jnp.concatenate([jnp.cos(ang_r)] * 2 + [jnp.cos(ang_c)] * 2, axis=-1)
    sin = jnp.concatenate([-jnp.sin(ang_r), jnp.sin(ang_r), -jnp.sin(ang_c), jnp.sin(ang_c)], axis=-1)
    cos = jnp.concatenate([jnp.ones((n_ctx, HEAD), F32), cos], axis=0)
    sin = jnp.concatenate([jnp.zeros((n_ctx, HEAD), F32), sin], axis=0)
    return cos, sin


def _lane_pad(vec, offset):
    out = jnp.zeros((1, LANES), F32)
    return lax.dynamic_update_slice(out, vec.reshape(1, -1).astype(F32), (0, offset))


def kernel(x, c, ctx, c_ctx, norm_g, ada_w, ada_b, dn_w_in, dn_conv_w, dn_a_log, dn_dt_bias,
           dn_o_norm_g, dn_w_out, att_w_in, att_q_norm_g, att_k_norm_g, att_w_out, final_norm_g):
    b, n_lat, d = x.shape
    n_ctx = ctx.shape[1]
    depth = norm_g.shape[0]
    assert n_ctx == ROW_TILE and n_lat % ROW_TILE == 0 and b < MOD_ROWS and d % LANES == 0
    lt = n_ctx + n_lat
    rep = DN_V_HEADS // DN_QK_HEADS

    cvec = jnp.concatenate([c, c_ctx[None, :], jnp.zeros((MOD_ROWS - b - 1, d), F32)], axis=0)
    mod4 = _modulation(cvec, ada_w, ada_b).reshape(depth, MOD_ROWS, 1, 3 * d)
    h = jnp.concatenate([ctx, x], axis=1)
    cos_t, sin_t = _rope_tables(n_lat, n_ctx)

    for i in range(depth):
        j = i // 2
        last = i == depth - 1
        g_i = norm_g[i].reshape(1, d)
        if i % 2 == 0:
            w_in = dn_w_in[j].astype(BF16)
            w_main = w_in[:, :DN_CONV_DIM + DN_VAL_DIM]
            w_ab = jnp.pad(w_in[:, DN_CONV_DIM + DN_VAL_DIM:], ((0, 0), (0, LANES - 4 * DN_V_HEADS)))
            qkv_pre, z, ab = _dn_inproj(h, mod4, g_i, w_main, w_ab, i)
            alog_vec = _lane_pad(dn_a_log[j], 2 * DN_V_HEADS)
            dtb_vec = _lane_pad(dn_dt_bias[j], 2 * DN_V_HEADS)
            qkv, gates = _dn_conv(qkv_pre, ab, dn_conv_w[j], alog_vec, dtb_vec)
            gt = gates[:, :, :4 * DN_V_HEADS].reshape(b, lt, 2, 2, DN_QK_HEADS, rep)
            gate_rows = jnp.transpose(gt[:, :, ::-1], (0, 3, 4, 2, 5, 1)).reshape(
                b, 2, DN_QK_HEADS, 2 * rep, lt)
            o_f, o_b = _delta_rule(qkv, gate_rows)
            h = _outproj([o_f, o_b], z, h, mod4, dn_w_out[j].astype(BF16), i,
                         head_g=dn_o_norm_g[j].reshape(1, HEAD),
                         final_g=final_norm_g.reshape(1, d) if last else None)
        else:
            w_in = att_w_in[j].astype(BF16)
            w_vt = w_in[:, ATT_Q_DIM + ATT_KV_DIM:ATT_Q_DIM + 2 * ATT_KV_DIM].T
            q, k, vt, z = _att_inproj(h, mod4, g_i, w_in, w_vt,
                                      att_q_norm_g[j].reshape(1, HEAD), att_k_norm_g[j].reshape(1, HEAD),
                                      cos_t, sin_t, i)
            o = _attention(q, k, vt)
            h = _outproj([o], z, h, mod4, att_w_out[j].astype(BF16), i,
                         final_g=final_norm_g.reshape(1, d) if last else None)
    return h
```

```python
import functools
import math

import jax
import jax.numpy as jnp
from jax import lax
from jax.experimental import pallas as pl
from jax.experimental.pallas import tpu as pltpu

F32 = jnp.float32
BF16 = jnp.bfloat16

NORM_EPS = 1e-6
GRID_W = 64
ROPE_THETA = 10000.0

HEAD = 128
DN_QK_HEADS = 8
DN_V_HEADS = 16
DN_KEY_DIM = DN_QK_HEADS * HEAD
DN_VAL_DIM = DN_V_HEADS * HEAD
DN_CONV_DIM = 2 * DN_KEY_DIM + DN_VAL_DIM
DN_CONV_K = 5
DN_CHUNK = 64
ATT_Q_HEADS = 8
ATT_KV_HEADS = 2
ATT_GROUP = ATT_Q_HEADS // ATT_KV_HEADS
ATT_Q_DIM = ATT_Q_HEADS * HEAD
ATT_KV_DIM = ATT_KV_HEADS * HEAD

ROW_TILE = 256
LANES = 128
HALO = 8
MOD_ROWS = 16
N_COL_CHUNK = 512
DN_HEAD_BLOCK = 2
DN_RANK_GROUP = 4
VMEM_LIMIT = 56 * 1024 * 1024
NEG_BIG = -1e30


def _cparams(*sem):
    return pltpu.CompilerParams(dimension_semantics=sem, vmem_limit_bytes=VMEM_LIMIT)


def _resident(shape):
    return pl.BlockSpec(shape, lambda *_: (0,) * len(shape), pipeline_mode=pl.Buffered(1))


def _silu(x):
    hx = 0.5 * x
    return hx + hx * jnp.tanh(hx)


def _mod_body(c_ref, w_ref, b_ref, o_ref):
    sc = _silu(c_ref[...])
    o_ref[...] = jnp.dot(sc, w_ref[...], preferred_element_type=F32) + b_ref[...]


def _modulation(cvec, ada_w, ada_b):
    depth, d, d3 = ada_w.shape
    tn = 1024
    return pl.pallas_call(
        _mod_body,
        grid=(depth, d3 // tn),
        in_specs=[
            pl.BlockSpec((MOD_ROWS, d), lambda i, j: (0, 0)),
            pl.BlockSpec((None, d, tn), lambda i, j: (i, 0, j)),
            pl.BlockSpec((None, 1, tn), lambda i, j: (i, 0, j)),
        ],
        out_specs=pl.BlockSpec((None, MOD_ROWS, tn), lambda i, j: (i, 0, j)),
        out_shape=jax.ShapeDtypeStruct((depth, MOD_ROWS, d3), F32),
        compiler_params=_cparams("arbitrary", "arbitrary"),
        name="adaln_modulation",
    )(cvec, ada_w, ada_b.reshape(depth, 1, d3))


def _mod_spec(layer, batch, d3):
    return pl.BlockSpec((None, None, 1, d3),
                        lambda b, r: (layer, jnp.where(r == 0, batch, b), 0, 0))


def _modulated_norm(x, g, mod, d):
    ms = jnp.mean(x * x, axis=-1, keepdims=True)
    xn = x * lax.rsqrt(ms + NORM_EPS) * g
    return xn * (1.0 + mod[:, d:2 * d]) + mod[:, 0:d]


def _dn_inproj_body(h_ref, mod_ref, g_ref, w_ref, wab_ref, qkv_ref, z_ref, ab_ref):
    d = h_ref.shape[-1]
    u = _modulated_norm(h_ref[...], g_ref[...], mod_ref[...], d).astype(BF16)
    for n in range(DN_CONV_DIM // N_COL_CHUNK):
        cs = slice(n * N_COL_CHUNK, (n + 1) * N_COL_CHUNK)
        qkv_ref[:, cs] = jnp.dot(u, w_ref[:, cs], preferred_element_type=F32)
    for n in range(DN_VAL_DIM // N_COL_CHUNK):
        cs = slice(n * N_COL_CHUNK, (n + 1) * N_COL_CHUNK)
        ws = slice(DN_CONV_DIM + n * N_COL_CHUNK, DN_CONV_DIM + (n + 1) * N_COL_CHUNK)
        z_ref[:, cs] = jnp.dot(u, w_ref[:, ws], preferred_element_type=F32).astype(BF16)
    ab_ref[...] = jnp.dot(u, wab_ref[...], preferred_element_type=F32)


def _dn_inproj(h, mod4, norm_g, w_main, w_ab, layer):
    b, lt, d = h.shape
    nt = lt // ROW_TILE
    d3 = mod4.shape[-1]
    row = lambda c: pl.BlockSpec((None, ROW_TILE, c), lambda bi, r: (bi, r, 0))
    return pl.pallas_call(
        _dn_inproj_body,
        grid=(b, nt),
        in_specs=[row(d), _mod_spec(layer, b, d3), _resident((1, d)),
                  _resident(w_main.shape), _resident(w_ab.shape)],
        out_specs=[row(DN_CONV_DIM), row(DN_VAL_DIM), row(LANES)],
        out_shape=[jax.ShapeDtypeStruct((b, lt, DN_CONV_DIM), F32),
                   jax.ShapeDtypeStruct((b, lt, DN_VAL_DIM), BF16),
                   jax.ShapeDtypeStruct((b, lt, LANES), F32)],
        compiler_params=_cparams("arbitrary", "arbitrary"),
        name="dn_inproj",
    )(h, mod4, norm_g, w_main, w_ab)


def _dn_conv_body(x_ref, prev_ref, next_ref, cw_ref, ab_ref, alog_ref, dtb_ref,
                  qkv_ref, gate_ref, pad_ref):
    r = pl.program_id(1)
    nt = pl.num_programs(1)
    has_prev = r > 1
    has_next = jnp.logical_and(r > 0, r < nt - 1)
    half = DN_CONV_K // 2
    for hb in range(DN_CONV_DIM // HEAD):
        cs = slice(hb * HEAD, (hb + 1) * HEAD)
        pad_ref[0:HALO, :] = jnp.where(has_prev, prev_ref[:, cs], 0.0)
        pad_ref[HALO:HALO + ROW_TILE, :] = x_ref[:, cs]
        pad_ref[HALO + ROW_TILE:, :] = jnp.where(has_next, next_ref[:, cs], 0.0)
        acc = None
        for j in range(DN_CONV_K):
            win = pad_ref[pl.ds(HALO - half + j, ROW_TILE), :]
            term = win * cw_ref[j:j + 1, cs]
            acc = term if acc is None else acc + term
        y = _silu(acc)
        if hb < 2 * DN_QK_HEADS:
            y = y * lax.rsqrt(jnp.sum(y * y, axis=-1, keepdims=True) + NORM_EPS)
        qkv_ref[:, cs] = y.astype(BF16)
    ab = ab_ref[...]
    lane = lax.broadcasted_iota(jnp.int32, ab.shape, 1)
    beta = jax.nn.sigmoid(ab)
    g = -jnp.exp(alog_ref[...]) * jax.nn.softplus(ab + dtb_ref[...])
    ti = lax.broadcasted_iota(jnp.int32, (ROW_TILE, ROW_TILE), 0)
    tj = lax.broadcasted_iota(jnp.int32, (ROW_TILE, ROW_TILE), 1)
    prefix_mat = jnp.where(jnp.logical_and(ti // DN_CHUNK == tj // DN_CHUNK, tj <= ti), 1.0, 0.0)
    prefix = jnp.dot(prefix_mat, g, precision=lax.Precision.HIGHEST, preferred_element_type=F32)
    row_chunk = lax.broadcasted_iota(jnp.int32, ab.shape, 0) // DN_CHUNK
    total = jnp.zeros_like(prefix)
    for ch in range(ROW_TILE // DN_CHUNK):
        end = (ch + 1) * DN_CHUNK
        total = jnp.where(row_chunk == ch, prefix[end - 1:end, :], total)
    backward_lane = lane >= 3 * DN_V_HEADS
    gcum = jnp.where(backward_lane, total - prefix + g, prefix)
    gate_ref[...] = jnp.where(lane < 2 * DN_V_HEADS, beta, gcum)


def _dn_conv(qkv_pre, ab, conv_w, alog_vec, dtb_vec):
    b, lt, c = qkv_pre.shape
    nt = lt // ROW_TILE
    per = ROW_TILE // HALO
    nhalo = lt // HALO
    row = lambda w: pl.BlockSpec((None, ROW_TILE, w), lambda bi, r: (bi, r, 0))
    return pl.pallas_call(
        _dn_conv_body,
        grid=(b, nt),
        in_specs=[
            row(c),
            pl.BlockSpec((None, HALO, c), lambda bi, r: (bi, jnp.maximum(r * per - 1, 0), 0)),
            pl.BlockSpec((None, HALO, c), lambda bi, r: (bi, jnp.minimum((r + 1) * per, nhalo - 1), 0)),
            _resident(conv_w.shape), row(LANES), _resident((1, LANES)), _resident((1, LANES)),
        ],
        out_specs=[row(c), row(LANES)],
        out_shape=[jax.ShapeDtypeStruct((b, lt, c), BF16), jax.ShapeDtypeStruct((b, lt, LANES), F32)],
        scratch_shapes=[pltpu.VMEM((ROW_TILE + 2 * HALO, HEAD), F32)],
        compiler_params=_cparams("arbitrary", "arbitrary"),
        name="dn_conv_gates",
    )(qkv_pre, qkv_pre, qkv_pre, conv_w, ab, alog_vec, dtb_vec)


def _dot(a, b):
    return jnp.dot(a.astype(BF16), b.astype(BF16), preferred_element_type=F32)


def _dot_nt(a, b):
    return lax.dot_general(a.astype(BF16), b.astype(BF16), (((1,), (1,)), ((), ())),
                           preferred_element_type=F32)


def _dot_tn(a, b):
    return lax.dot_general(a.astype(BF16), b.astype(BF16), (((0,), (0,)), ((), ())),
                           preferred_element_type=F32)


def _split(a):
    hi = a.astype(BF16)
    return hi, (a - hi.astype(F32)).astype(BF16)


def _pair_diag(x):
    lane = lax.broadcasted_iota(jnp.int32, x.shape, 1)
    zero = jnp.zeros_like(x)
    c = x.shape[0]
    return jnp.concatenate([jnp.where(lane < c, x, zero), jnp.where(lane >= c, x, zero)], axis=0)


def _pair_matmul(a, x):
    a_hi, a_lo = _split(a)
    x_hi, x_lo = _split(x)
    lhs = jnp.concatenate([a_hi, a_lo, a_hi], axis=1)
    rhs = jnp.concatenate([_pair_diag(x_hi), _pair_diag(x_hi), _pair_diag(x_lo)], axis=0)
    return jnp.dot(lhs, rhs, preferred_element_type=F32)


def _delta_body(qf_ref, kf_ref, vf_ref, gf_ref, qb_ref, kb_ref, vb_ref, gb_ref,
                of_ref, ob_ref, s_ref):
    c = DN_CHUNK
    n_chunks = ROW_TILE // c
    rep = DN_V_HEADS // DN_QK_HEADS
    assert rep == 2 and 2 * c == LANES
    q_scale = HEAD ** -0.5

    @pl.when(pl.program_id(2) == 0)
    def _():
        s_ref[...] = jnp.zeros_like(s_ref)

    ci = lax.broadcasted_iota(jnp.int32, (c, LANES), 0)
    cl = lax.broadcasted_iota(jnp.int32, (c, LANES), 1)
    cj = cl % c
    head1 = cl >= c
    eye_p = jnp.where(ci == cj, 1.0, 0.0)

    walks = []
    for reverse, (q_ref, k_ref, v_ref, g_ref, o_ref) in enumerate(
            ((qf_ref, kf_ref, vf_ref, gf_ref, of_ref), (qb_ref, kb_ref, vb_ref, gb_ref, ob_ref))):
        for hb in range(DN_HEAD_BLOCK):
            gr = g_ref[hb]
            gr8 = jnp.concatenate([gr, jnp.zeros((8 - gr.shape[0], ROW_TILE), F32)], axis=0)
            walks.append(dict(
                reverse=reverse, hb=hb, q_ref=q_ref, k_ref=k_ref, v_ref=v_ref, o_ref=o_ref,
                qk_cols=slice(hb * HEAD, (hb + 1) * HEAD), v_col0=hb * rep * HEAD,
                row=gr8, col=gr8.T,
                incl=(cj >= ci) if reverse else (cj <= ci),
                strict=(cj > ci) if reverse else (cj < ci),
                order=list(range(n_chunks - 1, -1, -1) if reverse else range(n_chunks))))

    def prepare(w, ch):
        rows = slice(ch * c, (ch + 1) * c)
        kc = w["k_ref"][rows, w["qk_cols"]]
        qc = w["q_ref"][rows, w["qk_cols"]]
        gcol = [w["col"][rows, s:s + 1] for s in range(rep)]
        bcol = [w["col"][rows, rep + s:rep + s + 1] for s in range(rep)]
        last = 0 if w["reverse"] else c - 1
        gtot = [g[last:last + 1, :] for g in gcol]
        grow_p = jnp.concatenate([w["row"][s:s + 1, rows] for s in range(rep)], axis=1)
        gcol_p = jnp.where(head1, gcol[1], gcol[0])
        beta_p = jnp.where(head1, bcol[1], bcol[0])
        decay_p = jnp.exp(jnp.where(w["incl"], gcol_p - grow_p, NEG_BIG))
        k2 = jnp.concatenate([kc, kc], axis=0)
        a_p = jnp.where(w["strict"], beta_p * _dot_nt(kc, k2) * decay_p, 0.0)
        qkm_p = q_scale * _dot_nt(qc, k2) * decay_p
        return dict(w=w, rows=rows, kf=kc.astype(F32), qf=qc.astype(F32), gcol=gcol,
                    bcol=bcol, gtot=gtot, a=a_p, qkm=qkm_p)

    def invert(items, between_levels=()):
        fill = list(between_levels)
        for it in items:
            it["x"] = _pair_matmul(it["a"], it["a"])
            it["t"] = eye_p - it["a"]
        for _ in range(int(math.log2(c)) - 2):
            if fill:
                fill.pop(0)()
            for it in items:
                r2 = _pair_matmul(jnp.concatenate([it["x"], it["t"]], axis=0), it["x"])
                it["x"] = r2[:c]
                it["t"] = it["t"] + r2[c:]
        for thunk in fill:
            thunk()
        for it in items:
            it["t"] = it["t"] + _pair_matmul(it["t"], it["x"])

    def chunk_products(items):
        for it in items:
            rhs = []
            for s in range(rep):
                eg = jnp.exp(it["gcol"][s])
                v0 = it["w"]["v_col0"] + s * HEAD
                vf = it["w"]["v_ref"][it["rows"], v0:v0 + HEAD].astype(F32)
                rhs.append(jnp.concatenate([vf * it["bcol"][s], it["kf"] * (it["bcol"][s] * eg)], axis=1))
            it["uw"] = _dot(_pair_diag(it["t"]), jnp.concatenate(rhs, axis=0))
        for it in items:
            it["qo"] = _dot(_pair_diag(it["qkm"]), it["uw"])
            it["gn"] = []
            for s in range(rep):
                k_dec = it["kf"] * jnp.exp(it["gtot"][s] - it["gcol"][s])
                it["gn"].append(_dot_tn(k_dec, it["uw"][s * c:(s + 1) * c]))

    def advance(items):
        for it in items:
            state = it["w"]["state"]
            for s in range(rep):
                q_eff = it["qf"] * (q_scale * jnp.exp(it["gcol"][s])) - it["qo"][s * c:(s + 1) * c, HEAD:]
                rs = _dot(jnp.concatenate([it["gn"][s][:, HEAD:], q_eff], axis=0), state[s])
                state[s] = state[s] * jnp.exp(it["gtot"][s]) - rs[:HEAD] + it["gn"][s][:, :HEAD]
                v0 = it["w"]["v_col0"] + s * HEAD
                it["w"]["o_ref"][it["rows"], v0:v0 + HEAD] = (
                    rs[HEAD:] + it["qo"][s * c:(s + 1) * c, :HEAD]).astype(BF16)

    for w in walks:
        w["state"] = [s_ref[w["reverse"], w["hb"], s] for s in range(rep)]
    groups = [[prepare(w, w["order"][rank]) for rank in range(first, first + DN_RANK_GROUP) for w in walks]
              for first in range(0, n_chunks, DN_RANK_GROUP)]
    pending = []
    for items in groups:
        invert(items, pending)
        chunk_products(items)
        pending = [functools.partial(advance, items[i:i + len(walks)])
                   for i in range(0, len(items), len(walks))]
    for thunk in pending:
        thunk()
    for w in walks:
        for s in range(rep):
            s_ref[w["reverse"], w["hb"], s] = w["state"][s]


def _delta_rule(qkv, gate_rows):
    b, lt, _ = qkv.shape
    nt = lt // ROW_TILE
    rep = DN_V_HEADS // DN_QK_HEADS
    v_col0 = 2 * DN_KEY_DIM // (DN_HEAD_BLOCK * rep * HEAD)
    fwd = lambda r: r
    bwd = lambda r: jnp.where(r == 0, 0, nt - r)

    hb = DN_HEAD_BLOCK
    n_blocks = DN_QK_HEADS // hb

    def specs(tile, d):
        return [
            pl.BlockSpec((None, ROW_TILE, hb * HEAD), lambda bi, j, r: (bi, tile(r), j)),
            pl.BlockSpec((None, ROW_TILE, hb * HEAD), lambda bi, j, r: (bi, tile(r), n_blocks + j)),
            pl.BlockSpec((None, ROW_TILE, hb * rep * HEAD), lambda bi, j, r: (bi, tile(r), v_col0 + j)),
            pl.BlockSpec((None, None, hb, 2 * rep, ROW_TILE), lambda bi, j, r: (bi, d, j, 0, tile(r))),
        ]

    out = lambda tile: pl.BlockSpec((None, ROW_TILE, hb * rep * HEAD), lambda bi, j, r: (bi, tile(r), j))
    o_shape = jax.ShapeDtypeStruct((b, lt, DN_VAL_DIM), BF16)
    return pl.pallas_call(
        _delta_body,
        grid=(b, n_blocks, nt),
        in_specs=specs(fwd, 0) + specs(bwd, 1),
        out_specs=[out(fwd), out(bwd)],
        out_shape=[o_shape, o_shape],
        scratch_shapes=[pltpu.VMEM((2, hb, rep, HEAD, HEAD), F32)],
        compiler_params=_cparams("arbitrary", "arbitrary", "arbitrary"),
        name="dn_delta_rule",
    )(qkv, qkv, qkv, gate_rows, qkv, qkv, qkv, gate_rows)


def _outproj_body(*refs, n_o, head_norm, final_norm):
    o_refs = refs[:n_o]
    z_ref, h_ref, mod_ref, w_ref = refs[n_o:n_o + 4]
    rest = list(refs[n_o + 4:])
    og_ref = rest.pop(0) if head_norm else None
    fg_ref = rest.pop(0) if final_norm else None
    out_ref = rest.pop(0)
    d = h_ref.shape[-1]
    width = z_ref.shape[-1]
    parts = []
    for hb in range(width // HEAD):
        cs = slice(hb * HEAD, (hb + 1) * HEAD)
        o = o_refs[0][:, cs].astype(F32)
        for extra in o_refs[1:]:
            o = o + extra[:, cs].astype(F32)
        if head_norm:
            o = o * lax.rsqrt(jnp.mean(o * o, axis=-1, keepdims=True) + NORM_EPS) * og_ref[...]
        parts.append((o * _silu(z_ref[:, cs].astype(F32))).astype(BF16))
    y = jnp.dot(jnp.concatenate(parts, axis=1), w_ref[...], preferred_element_type=F32)
    hn = h_ref[...] + mod_ref[:, 2 * d:3 * d] * y
    if final_norm:
        hn = hn * lax.rsqrt(jnp.mean(hn * hn, axis=-1, keepdims=True) + NORM_EPS) * fg_ref[...]
    out_ref[...] = hn


def _outproj(o_list, z, h, mod4, w_out, layer, head_g=None, final_g=None):
    b, lt, d = h.shape
    nt = lt // ROW_TILE
    d3 = mod4.shape[-1]
    width = z.shape[-1]
    skip = 1 if final_g is not None else 0
    row = lambda c: pl.BlockSpec((None, ROW_TILE, c), lambda bi, r: (bi, r + skip, 0))
    mod_spec = pl.BlockSpec((None, None, 1, d3),
                            lambda bi, r: (layer, jnp.where(r + skip == 0, b, bi), 0, 0))
    in_specs = [row(width)] * len(o_list) + [row(width), row(d), mod_spec, _resident(w_out.shape)]
    args = list(o_list) + [z, h, mod4, w_out]
    if head_g is not None:
        in_specs.append(_resident((1, HEAD)))
        args.append(head_g)
    if final_g is not None:
        in_specs.append(_resident((1, d)))
        args.append(final_g)
    return pl.pallas_call(
        functools.partial(_outproj_body, n_o=len(o_list), head_norm=head_g is not None,
                          final_norm=final_g is not None),
        grid=(b, nt - skip),
        in_specs=in_specs,
        out_specs=pl.BlockSpec((None, ROW_TILE, d), lambda bi, r: (bi, r, 0)),
        out_shape=jax.ShapeDtypeStruct((b, lt - skip * ROW_TILE, d), F32),
        compiler_params=_cparams("arbitrary", "arbitrary"),
        name="mixer_outproj",
    )(*args)


def _rope_partner(x):
    lane = lax.broadcasted_iota(jnp.int32, x.shape, 1)
    quarter = HEAD // 4
    return jnp.where((lane % (2 * quarter)) < quarter,
                     pltpu.roll(x, HEAD - quarter, axis=1), pltpu.roll(x, quarter, axis=1))


def _att_inproj_body(h_ref, mod_ref, g_ref, w_ref, wvt_ref, qg_ref, kg_ref, cos_ref, sin_ref,
                     q_ref, k_ref, vt_ref, z_ref):
    d = h_ref.shape[-1]
    u = _modulated_norm(h_ref[...], g_ref[...], mod_ref[...], d).astype(BF16)
    cos = cos_ref[...]
    sin = sin_ref[...]
    q_scale = (HEAD ** -0.5) * math.log2(math.e)

    def normed_rope(x, g):
        xn = x * lax.rsqrt(jnp.mean(x * x, axis=-1, keepdims=True) + NORM_EPS) * g
        return xn * cos + _rope_partner(xn) * sin

    pair = 2 * HEAD
    for pb in range(ATT_Q_DIM // pair):
        x2 = jnp.dot(u, w_ref[:, pb * pair:(pb + 1) * pair], preferred_element_type=F32)
        for t in range(2):
            cs = slice(pb * pair + t * HEAD, pb * pair + (t + 1) * HEAD)
            q_ref[:, cs] = (normed_rope(x2[:, t * HEAD:(t + 1) * HEAD], qg_ref[...]) * q_scale).astype(BF16)
    for pb in range(ATT_KV_DIM // pair):
        x2 = jnp.dot(u, w_ref[:, ATT_Q_DIM + pb * pair:ATT_Q_DIM + (pb + 1) * pair],
                     preferred_element_type=F32)
        for t in range(2):
            cs = slice(pb * pair + t * HEAD, pb * pair + (t + 1) * HEAD)
            k_ref[:, cs] = normed_rope(x2[:, t * HEAD:(t + 1) * HEAD], kg_ref[...]).astype(BF16)
    vt = lax.dot_general(wvt_ref[...], u, (((1,), (1,)), ((), ())), preferred_element_type=F32)
    for hk in range(ATT_KV_HEADS):
        vt_ref[hk] = vt[hk * HEAD:(hk + 1) * HEAD, :].astype(BF16)
    z0 = ATT_Q_DIM + 2 * ATT_KV_DIM
    for n in range(ATT_Q_DIM // N_COL_CHUNK):
        cs = slice(n * N_COL_CHUNK, (n + 1) * N_COL_CHUNK)
        ws = slice(z0 + n * N_COL_CHUNK, z0 + (n + 1) * N_COL_CHUNK)
        z_ref[:, cs] = jnp.dot(u, w_ref[:, ws], preferred_element_type=F32).astype(BF16)


def _att_inproj(h, mod4, norm_g, w_in, w_vt, q_g, k_g, cos_t, sin_t, layer):
    b, lt, d = h.shape
    nt = lt // ROW_TILE
    d3 = mod4.shape[-1]
    row = lambda c: pl.BlockSpec((None, ROW_TILE, c), lambda bi, r: (bi, r, 0))
    tab = pl.BlockSpec((ROW_TILE, HEAD), lambda bi, r: (r, 0))
    return pl.pallas_call(
        _att_inproj_body,
        grid=(b, nt),
        in_specs=[row(d), _mod_spec(layer, b, d3), _resident((1, d)), _resident(w_in.shape),
                  _resident(w_vt.shape), _resident((1, HEAD)), _resident((1, HEAD)), tab, tab],
        out_specs=[row(ATT_Q_DIM), row(ATT_KV_DIM),
                   pl.BlockSpec((None, ATT_KV_HEADS, None, HEAD, ROW_TILE), lambda bi, r: (bi, 0, r, 0, 0)),
                   row(ATT_Q_DIM)],
        out_shape=[jax.ShapeDtypeStruct((b, lt, ATT_Q_DIM), BF16),
                   jax.ShapeDtypeStruct((b, lt, ATT_KV_DIM), BF16),
                   jax.ShapeDtypeStruct((b, ATT_KV_HEADS, nt, HEAD, ROW_TILE), BF16),
                   jax.ShapeDtypeStruct((b, lt, ATT_Q_DIM), BF16)],
        compiler_params=_cparams("arbitrary", "arbitrary"),
        name="att_inproj",
    )(h, mod4, norm_g, w_in, w_vt, q_g, k_g, cos_t, sin_t)


def _attn_body(q_ref, k_ref, vt_ref, o_ref, s_ref):
    sub = 8
    fold = lambda x, op: op(x.reshape(ROW_TILE // sub, sub, ROW_TILE), axis=0)

    def score_chunk(g, m, chunk):
        st = lax.dot_general(k_ref[chunk * ROW_TILE:(chunk + 1) * ROW_TILE, :],
                             q_ref[:, g * HEAD:(g + 1) * HEAD],
                             (((1,), (1,)), ((), ())), preferred_element_type=F32)
        s_ref[g % 2, chunk] = st
        return jnp.maximum(m, fold(st, jnp.max))

    def value_chunk(g, m_row, l, acc, chunk):
        pt = jnp.exp2(s_ref[g % 2, chunk] - m_row)
        l = l + fold(pt, jnp.sum)
        return l, acc + jnp.dot(vt_ref[chunk], pt.astype(BF16), preferred_element_type=F32)

    def phase(n_chunks, g_score, g_value, m_row):
        m = jnp.full((sub, ROW_TILE), NEG_BIG, F32)
        l = jnp.zeros((sub, ROW_TILE), F32)
        acc = jnp.zeros((HEAD, ROW_TILE), F32)
        for chunk in range(n_chunks):
            if g_score is not None:
                m = score_chunk(g_score, m, chunk)
            if g_value is not None:
                l, acc = value_chunk(g_value, m_row, l, acc, chunk)
        if g_value is not None:
            o_t = acc / jnp.sum(l, axis=0, keepdims=True)
            o_ref[:, g_value * HEAD:(g_value + 1) * HEAD] = o_t.T.astype(BF16)
        return jnp.max(m, axis=0, keepdims=True) if g_score is not None else None

    def attend(n_chunks):
        m_row = phase(n_chunks, 0, None, None)
        for g in range(ATT_GROUP):
            m_row = phase(n_chunks, g + 1 if g + 1 < ATT_GROUP else None, g, m_row)

    is_ctx = pl.program_id(2) == 0
    pl.when(is_ctx)(functools.partial(attend, 1))
    pl.when(jnp.logical_not(is_ctx))(functools.partial(attend, k_ref.shape[0] // ROW_TILE))


def _attention(q, k, vt):
    b, lt, _ = q.shape
    nt = lt // ROW_TILE
    gw = ATT_GROUP * HEAD
    return pl.pallas_call(
        _attn_body,
        grid=(b, ATT_KV_HEADS, nt),
        in_specs=[
            pl.BlockSpec((None, ROW_TILE, gw), lambda bi, hk, r: (bi, r, hk)),
            pl.BlockSpec((None, lt, HEAD), lambda bi, hk, r: (bi, 0, hk)),
            pl.BlockSpec((None, None, nt, HEAD, ROW_TILE), lambda bi, hk, r: (bi, hk, 0, 0, 0)),
        ],
        out_specs=pl.BlockSpec((None, ROW_TILE, gw), lambda bi, hk, r: (bi, r, hk)),
        out_shape=jax.ShapeDtypeStruct((b, lt, ATT_Q_DIM), BF16),
        scratch_shapes=[pltpu.VMEM((2, nt, ROW_TILE, ROW_TILE), F32)],
        compiler_params=_cparams("arbitrary", "arbitrary", "arbitrary"),
        name="gqa_attention",
    )(q, k, vt)


def _rope_tables(n_lat, n_ctx):
    t = jnp.arange(n_lat)
    row = (t // GRID_W).astype(F32)
    col = (t % GRID_W).astype(F32)
    axis_dim = HEAD // 2
    inv = ROPE_THETA ** (-jnp.arange(0, axis_dim, 2, dtype=F32) / axis_dim)
    ang_r = row[:, None] * inv
    ang_c = col[:, None] * inv
    cos = jnp.concatenate([jnp.cos(ang_r)] * 2 + [jnp.cos(ang_c)] * 2, axis=-1)
    sin = jnp.concatenate([-jnp.sin(ang_r), jnp.sin(ang_r), -jnp.sin(ang_c), jnp.sin(ang_c)], axis=-1)
    cos = jnp.concatenate([jnp.ones((n_ctx, HEAD), F32), cos], axis=0)
    sin = jnp.concatenate([jnp.zeros((n_ctx, HEAD), F32), sin], axis=0)
    return cos, sin


def _lane_pad(vec, offset):
    out = jnp.zeros((1, LANES), F32)
    return lax.dynamic_update_slice(out, vec.reshape(1, -1).astype(F32), (0, offset))


def kernel(x, c, ctx, c_ctx, norm_g, ada_w, ada_b, dn_w_in, dn_conv_w, dn_a_log, dn_dt_bias,
           dn_o_norm_g, dn_w_out, att_w_in, att_q_norm_g, att_k_norm_g, att_w_out, final_norm_g):
    b, n_lat, d = x.shape
    n_ctx = ctx.shape[1]
    depth = norm_g.shape[0]
    assert n_ctx == ROW_TILE and n_lat % ROW_TILE == 0 and b < MOD_ROWS and d % LANES == 0
    lt = n_ctx + n_lat
    rep = DN_V_HEADS // DN_QK_HEADS

    cvec = jnp.concatenate([c, c_ctx[None, :], jnp.zeros((MOD_ROWS - b - 1, d), F32)], axis=0)
    mod4 = _modulation(cvec, ada_w, ada_b).reshape(depth, MOD_ROWS, 1, 3 * d)
    h = jnp.concatenate([ctx, x], axis=1)
    cos_t, sin_t = _rope_tables(n_lat, n_ctx)

    for i in range(depth):
        j = i // 2
        last = i == depth - 1
        g_i = norm_g[i].reshape(1, d)
        if i % 2 == 0:
            w_in = dn_w_in[j].astype(BF16)
            w_main = w_in[:, :DN_CONV_DIM + DN_VAL_DIM]
            w_ab = jnp.pad(w_in[:, DN_CONV_DIM + DN_VAL_DIM:], ((0, 0), (0, LANES - 4 * DN_V_HEADS)))
            qkv_pre, z, ab = _dn_inproj(h, mod4, g_i, w_main, w_ab, i)
            alog_vec = _lane_pad(dn_a_log[j], 2 * DN_V_HEADS)
            dtb_vec = _lane_pad(dn_dt_bias[j], 2 * DN_V_HEADS)
            qkv, gates = _dn_conv(qkv_pre, ab, dn_conv_w[j], alog_vec, dtb_vec)
            gt = gates[:, :, :4 * DN_V_HEADS].reshape(b, lt, 2, 2, DN_QK_HEADS, rep)
            gate_rows = jnp.transpose(gt[:, :, ::-1], (0, 3, 4, 2, 5, 1)).reshape(
                b, 2, DN_QK_HEADS, 2 * rep, lt)
            o_f, o_b = _delta_rule(qkv, gate_rows)
            h = _outproj([o_f, o_b], z, h, mod4, dn_w_out[j].astype(BF16), i,
                         head_g=dn_o_norm_g[j].reshape(1, HEAD),
                         final_g=final_norm_g.reshape(1, d) if last else None)
        else:
            w_in = att_w_in[j].astype(BF16)
            w_vt = w_in[:, ATT_Q_DIM + ATT_KV_DIM:ATT_Q_DIM + 2 * ATT_KV_DIM].T
            q, k, vt, z = _att_inproj(h, mod4, g_i, w_in, w_vt,
                                      att_q_norm_g[j].reshape(1, HEAD), att_k_norm_g[j].reshape(1, HEAD),
                                      cos_t, sin_t, i)
            o = _attention(q, k, vt)
            h = _outproj([o], z, h, mod4, att_w_out[j].astype(BF16), i,
                         final_g=final_norm_g.reshape(1, d) if last else None)
    return h
```

```python
import functools
import math

import jax
import jax.numpy as jnp
from jax import lax
from jax.experimental import pallas as pl
from jax.experimental.pallas import tpu as pltpu

F32 = jnp.float32
BF16 = jnp.bfloat16

NORM_EPS = 1e-6
GRID_W = 64
ROPE_THETA = 10000.0

HEAD = 128
DN_QK_HEADS = 8
DN_V_HEADS = 16
DN_KEY_DIM = DN_QK_HEADS * HEAD
DN_VAL_DIM = DN_V_HEADS * HEAD
DN_CONV_DIM = 2 * DN_KEY_DIM + DN_VAL_DIM
DN_CONV_K = 5
DN_CHUNK = 64
ATT_Q_HEADS = 8
ATT_KV_HEADS = 2
ATT_GROUP = ATT_Q_HEADS // ATT_KV_HEADS
ATT_Q_DIM = ATT_Q_HEADS * HEAD
ATT_KV_DIM = ATT_KV_HEADS * HEAD

ROW_TILE = 256
LANES = 128
HALO = 8
MOD_ROWS = 16
N_COL_CHUNK = 512
DN_HEAD_BLOCK = 2
VMEM_LIMIT = 56 * 1024 * 1024
NEG_BIG = -1e30


def _cparams(*sem):
    return pltpu.CompilerParams(dimension_semantics=sem, vmem_limit_bytes=VMEM_LIMIT)


def _resident(shape):
    return pl.BlockSpec(shape, lambda *_: (0,) * len(shape), pipeline_mode=pl.Buffered(1))


def _silu(x):
    hx = 0.5 * x
    return hx + hx * jnp.tanh(hx)


def _mod_body(c_ref, w_ref, b_ref, o_ref):
    sc = _silu(c_ref[...])
    o_ref[...] = jnp.dot(sc, w_ref[...], preferred_element_type=F32) + b_ref[...]


def _modulation(cvec, ada_w, ada_b):
    depth, d, d3 = ada_w.shape
    tn = 1024
    return pl.pallas_call(
        _mod_body,
        grid=(depth, d3 // tn),
        in_specs=[
            pl.BlockSpec((MOD_ROWS, d), lambda i, j: (0, 0)),
            pl.BlockSpec((None, d, tn), lambda i, j: (i, 0, j)),
            pl.BlockSpec((None, 1, tn), lambda i, j: (i, 0, j)),
        ],
        out_specs=pl.BlockSpec((None, MOD_ROWS, tn), lambda i, j: (i, 0, j)),
        out_shape=jax.ShapeDtypeStruct((depth, MOD_ROWS, d3), F32),
        compiler_params=_cparams("arbitrary", "arbitrary"),
        name="adaln_modulation",
    )(cvec, ada_w, ada_b.reshape(depth, 1, d3))


def _mod_spec(layer, batch, d3):
    return pl.BlockSpec((None, None, 1, d3),
                        lambda b, r: (layer, jnp.where(r == 0, batch, b), 0, 0))


def _modulated_norm(x, g, mod, d):
    ms = jnp.mean(x * x, axis=-1, keepdims=True)
    xn = x * lax.rsqrt(ms + NORM_EPS) * g
    return xn * (1.0 + mod[:, d:2 * d]) + mod[:, 0:d]


def _dn_inproj_body(h_ref, mod_ref, g_ref, w_ref, wab_ref, qkv_ref, z_ref, ab_ref):
    d = h_ref.shape[-1]
    u = _modulated_norm(h_ref[...], g_ref[...], mod_ref[...], d).astype(BF16)
    for n in range(DN_CONV_DIM // N_COL_CHUNK):
        cs = slice(n * N_COL_CHUNK, (n + 1) * N_COL_CHUNK)
        qkv_ref[:, cs] = jnp.dot(u, w_ref[:, cs], preferred_element_type=F32)
    for n in range(DN_VAL_DIM // N_COL_CHUNK):
        cs = slice(n * N_COL_CHUNK, (n + 1) * N_COL_CHUNK)
        ws = slice(DN_CONV_DIM + n * N_COL_CHUNK, DN_CONV_DIM + (n + 1) * N_COL_CHUNK)
        z_ref[:, cs] = jnp.dot(u, w_ref[:, ws], preferred_element_type=F32).astype(BF16)
    ab_ref[...] = jnp.dot(u, wab_ref[...], preferred_element_type=F32)


def _dn_inproj(h, mod4, norm_g, w_main, w_ab, layer):
    b, lt, d = h.shape
    nt = lt // ROW_TILE
    d3 = mod4.shape[-1]
    row = lambda c: pl.BlockSpec((None, ROW_TILE, c), lambda bi, r: (bi, r, 0))
    return pl.pallas_call(
        _dn_inproj_body,
        grid=(b, nt),
        in_specs=[row(d), _mod_spec(layer, b, d3), _resident((1, d)),
                  _resident(w_main.shape), _resident(w_ab.shape)],
        out_specs=[row(DN_CONV_DIM), row(DN_VAL_DIM), row(LANES)],
        out_shape=[jax.ShapeDtypeStruct((b, lt, DN_CONV_DIM), F32),
                   jax.ShapeDtypeStruct((b, lt, DN_VAL_DIM), BF16),
                   jax.ShapeDtypeStruct((b, lt, LANES), F32)],
        compiler_params=_cparams("arbitrary", "arbitrary"),
        name="dn_inproj",
    )(h, mod4, norm_g, w_main, w_ab)


def _dn_conv_body(x_ref, prev_ref, next_ref, cw_ref, ab_ref, alog_ref, dtb_ref,
                  qkv_ref, gate_ref, pad_ref):
    r = pl.program_id(1)
    nt = pl.num_programs(1)
    has_prev = r > 1
    has_next = jnp.logical_and(r > 0, r < nt - 1)
    half = DN_CONV_K // 2
    for hb in range(DN_CONV_DIM // HEAD):
        cs = slice(hb * HEAD, (hb + 1) * HEAD)
        pad_ref[0:HALO, :] = jnp.where(has_prev, prev_ref[:, cs], 0.0)
        pad_ref[HALO:HALO + ROW_TILE, :] = x_ref[:, cs]
        pad_ref[HALO + ROW_TILE:, :] = jnp.where(has_next, next_ref[:, cs], 0.0)
        acc = None
        for j in range(DN_CONV_K):
            win = pad_ref[pl.ds(HALO - half + j, ROW_TILE), :]
            term = win * cw_ref[j:j + 1, cs]
            acc = term if acc is None else acc + term
        y = _silu(acc)
        if hb < 2 * DN_QK_HEADS:
            y = y * lax.rsqrt(jnp.sum(y * y, axis=-1, keepdims=True) + NORM_EPS)
        qkv_ref[:, cs] = y.astype(BF16)
    ab = ab_ref[...]
    lane = lax.broadcasted_iota(jnp.int32, ab.shape, 1)
    beta = jax.nn.sigmoid(ab)
    g = -jnp.exp(alog_ref[...]) * jax.nn.softplus(ab + dtb_ref[...])
    ti = lax.broadcasted_iota(jnp.int32, (ROW_TILE, ROW_TILE), 0)
    tj = lax.broadcasted_iota(jnp.int32, (ROW_TILE, ROW_TILE), 1)
    prefix_mat = jnp.where(jnp.logical_and(ti // DN_CHUNK == tj // DN_CHUNK, tj <= ti), 1.0, 0.0)
    prefix = jnp.dot(prefix_mat, g, precision=lax.Precision.HIGHEST, preferred_element_type=F32)
    row_chunk = lax.broadcasted_iota(jnp.int32, ab.shape, 0) // DN_CHUNK
    total = jnp.zeros_like(prefix)
    for ch in range(ROW_TILE // DN_CHUNK):
        end = (ch + 1) * DN_CHUNK
        total = jnp.where(row_chunk == ch, prefix[end - 1:end, :], total)
    backward_lane = lane >= 3 * DN_V_HEADS
    gcum = jnp.where(backward_lane, total - prefix + g, prefix)
    gate_ref[...] = jnp.where(lane < 2 * DN_V_HEADS, beta, gcum)


def _dn_conv(qkv_pre, ab, conv_w, alog_vec, dtb_vec):
    b, lt, c = qkv_pre.shape
    nt = lt // ROW_TILE
    per = ROW_TILE // HALO
    nhalo = lt // HALO
    row = lambda w: pl.BlockSpec((None, ROW_TILE, w), lambda bi, r: (bi, r, 0))
    return pl.pallas_call(
        _dn_conv_body,
        grid=(b, nt),
        in_specs=[
            row(c),
            pl.BlockSpec((None, HALO, c), lambda bi, r: (bi, jnp.maximum(r * per - 1, 0), 0)),
            pl.BlockSpec((None, HALO, c), lambda bi, r: (bi, jnp.minimum((r + 1) * per, nhalo - 1), 0)),
            _resident(conv_w.shape), row(LANES), _resident((1, LANES)), _resident((1, LANES)),
        ],
        out_specs=[row(c), row(LANES)],
        out_shape=[jax.ShapeDtypeStruct((b, lt, c), BF16), jax.ShapeDtypeStruct((b, lt, LANES), F32)],
        scratch_shapes=[pltpu.VMEM((ROW_TILE + 2 * HALO, HEAD), F32)],
        compiler_params=_cparams("arbitrary", "arbitrary"),
        name="dn_conv_gates",
    )(qkv_pre, qkv_pre, qkv_pre, conv_w, ab, alog_vec, dtb_vec)


def _dot(a, b):
    return jnp.dot(a.astype(BF16), b.astype(BF16), preferred_element_type=F32)


def _dot_nt(a, b):
    return lax.dot_general(a.astype(BF16), b.astype(BF16), (((1,), (1,)), ((), ())),
                           preferred_element_type=F32)


def _dot_tn(a, b):
    return lax.dot_general(a.astype(BF16), b.astype(BF16), (((0,), (0,)), ((), ())),
                           preferred_element_type=F32)


def _pair_diag(x):
    lane = lax.broadcasted_iota(jnp.int32, x.shape, 1)
    zero = jnp.zeros_like(x)
    c = x.shape[0]
    return jnp.concatenate([jnp.where(lane < c, x, zero), jnp.where(lane >= c, x, zero)], axis=0)


def _pair_matmul(a, x):
    return jnp.dot(a.astype(BF16), _pair_diag(x.astype(BF16)), preferred_element_type=F32)


def _delta_body(qf_ref, kf_ref, vf_ref, gf_ref, qb_ref, kb_ref, vb_ref, gb_ref,
                of_ref, ob_ref, s_ref):
    c = DN_CHUNK
    n_chunks = ROW_TILE // c
    rep = DN_V_HEADS // DN_QK_HEADS
    assert rep == 2 and 2 * c == LANES
    q_scale = HEAD ** -0.5

    @pl.when(pl.program_id(2) == 0)
    def _():
        s_ref[...] = jnp.zeros_like(s_ref)

    ci = lax.broadcasted_iota(jnp.int32, (c, LANES), 0)
    cl = lax.broadcasted_iota(jnp.int32, (c, LANES), 1)
    cj = cl % c
    head1 = cl >= c
    eye_p = jnp.where(ci == cj, 1.0, 0.0)

    walks = []
    for reverse, (q_ref, k_ref, v_ref, g_ref, o_ref) in enumerate(
            ((qf_ref, kf_ref, vf_ref, gf_ref, of_ref), (qb_ref, kb_ref, vb_ref, gb_ref, ob_ref))):
        for hb in range(DN_HEAD_BLOCK):
            gr = g_ref[hb]
            gr8 = jnp.concatenate([gr, jnp.zeros((8 - gr.shape[0], ROW_TILE), F32)], axis=0)
            walks.append(dict(
                reverse=reverse, hb=hb, q_ref=q_ref, k_ref=k_ref, v_ref=v_ref, o_ref=o_ref,
                qk_cols=slice(hb * HEAD, (hb + 1) * HEAD), v_col0=hb * rep * HEAD,
                row=gr8, col=gr8.T,
                incl=(cj >= ci) if reverse else (cj <= ci),
                strict=(cj > ci) if reverse else (cj < ci),
                order=list(range(n_chunks - 1, -1, -1) if reverse else range(n_chunks))))

    def prepare(pairs):
        items = []
        for w, ch in pairs:
            rows = slice(ch * c, (ch + 1) * c)
            gcol = [w["col"][rows, s:s + 1] for s in range(rep)]
            bcol = [w["col"][rows, rep + s:rep + s + 1] for s in range(rep)]
            last = 0 if w["reverse"] else c - 1
            items.append(dict(w=w, rows=rows, gcol=gcol, bcol=bcol,
                              gtot=[g[last:last + 1, :] for g in gcol]))
        for it in items:
            w, rows = it["w"], it["rows"]
            kc = w["k_ref"][rows, w["qk_cols"]]
            qc = w["q_ref"][rows, w["qk_cols"]]
            k2 = jnp.concatenate([kc, kc], axis=0)
            it.update(kf=kc.astype(F32), qf=qc.astype(F32), kk=_dot_nt(kc, k2), qk=_dot_nt(qc, k2))
        for it in items:
            w = it["w"]
            grow_p = jnp.concatenate([w["row"][s:s + 1, it["rows"]] for s in range(rep)], axis=1)
            gcol_p = jnp.where(head1, it["gcol"][1], it["gcol"][0])
            it["beta_p"] = jnp.where(head1, it["bcol"][1], it["bcol"][0])
            it["decay_p"] = jnp.exp(jnp.where(w["incl"], gcol_p - grow_p, NEG_BIG))
        for it in items:
            it["a"] = jnp.where(it["w"]["strict"], it["beta_p"] * it.pop("kk") * it["decay_p"], 0.0)
            it["qkm"] = q_scale * it.pop("qk") * it.pop("decay_p")
        return items

    def invert(items):
        for it in items:
            it["x"] = _pair_matmul(it["a"], it["a"])
            it["t"] = eye_p - it["a"]
        for _ in range(int(math.log2(c)) - 2):
            for it in items:
                r2 = _pair_matmul(jnp.concatenate([it["x"], it["t"]], axis=0), it["x"])
                it["x"] = r2[:c]
                it["t"] = it["t"] + r2[c:]
        for it in items:
            it["t"] = it["t"] + _pair_matmul(it["t"], it["x"])
        for it in items:
            it["res"] = eye_p - it["t"] - _pair_matmul(it["a"], it["t"])
        for it in items:
            it["t"] = it["t"] + _pair_matmul(it["t"], it.pop("res"))

    def chunk_products(items):
        for it in items:
            rhs = []
            for s in range(rep):
                eg = jnp.exp(it["gcol"][s])
                v0 = it["w"]["v_col0"] + s * HEAD
                vf = it["w"]["v_ref"][it["rows"], v0:v0 + HEAD].astype(F32)
                rhs.append(jnp.concatenate([vf * it["bcol"][s], it["kf"] * (it["bcol"][s] * eg)], axis=1))
            it["rhs"] = jnp.concatenate(rhs, axis=0).astype(BF16)
            it["k_dec"] = [(it["kf"] * jnp.exp(it["gtot"][s] - it["gcol"][s])).astype(BF16) for s in range(rep)]
            it["qd"] = [it["qf"] * (q_scale * jnp.exp(it["gcol"][s])) for s in range(rep)]
            it["e_tot"] = [jnp.exp(it["gtot"][s]) for s in range(rep)]
        for it in items:
            it["uw"] = _dot(_pair_diag(it["t"]), it.pop("rhs"))
        for it in items:
            it["qo"] = _dot(_pair_diag(it["qkm"]), it["uw"])
        for it in items:
            it["gn"] = [_dot_tn(it["k_dec"][s], it["uw"][s * c:(s + 1) * c])
                        for s in range(rep)]
        for it in items:
            it["q_eff"] = [it["qd"][s] - it["qo"][s * c:(s + 1) * c, HEAD:] for s in range(rep)]

    def advance(items):
        for it in items:
            state = it["w"]["state"]
            for s in range(rep):
                rs = _dot(jnp.concatenate([it["gn"][s][:, HEAD:], it["q_eff"][s]], axis=0), state[s])
                state[s] = state[s] * it["e_tot"][s] - rs[:HEAD] + it["gn"][s][:, :HEAD]
                v0 = it["w"]["v_col0"] + s * HEAD
                it["w"]["o_ref"][it["rows"], v0:v0 + HEAD] = (
                    rs[HEAD:] + it["qo"][s * c:(s + 1) * c, :HEAD]).astype(BF16)

    for w in walks:
        w["state"] = [s_ref[w["reverse"], w["hb"], s] for s in range(rep)]
    items = prepare([(w, w["order"][rank]) for rank in range(n_chunks) for w in walks])
    invert(items)
    chunk_products(items)
    for rank in range(n_chunks):
        advance(items[rank * len(walks):(rank + 1) * len(walks)])
    for w in walks:
        for s in range(rep):
            s_ref[w["reverse"], w["hb"], s] = w["state"][s]


def _delta_rule(qkv, gate_rows):
    b, lt, _ = qkv.shape
    nt = lt // ROW_TILE
    rep = DN_V_HEADS // DN_QK_HEADS
    v_col0 = 2 * DN_KEY_DIM // (DN_HEAD_BLOCK * rep * HEAD)
    fwd = lambda r: r
    bwd = lambda r: jnp.where(r == 0, 0, nt - r)

    hb = DN_HEAD_BLOCK
    n_blocks = DN_QK_HEADS // hb

    def specs(tile, d):
        return [
            pl.BlockSpec((None, ROW_TILE, hb * HEAD), lambda bi, j, r: (bi, tile(r), j)),
            pl.BlockSpec((None, ROW_TILE, hb * HEAD), lambda bi, j, r: (bi, tile(r), n_blocks + j)),
            pl.BlockSpec((None, ROW_TILE, hb * rep * HEAD), lambda bi, j, r: (bi, tile(r), v_col0 + j)),
            pl.BlockSpec((None, None, hb, 2 * rep, ROW_TILE), lambda bi, j, r: (bi, d, j, 0, tile(r))),
        ]

    out = lambda tile: pl.BlockSpec((None, ROW_TILE, hb * rep * HEAD), lambda bi, j, r: (bi, tile(r), j))
    o_shape = jax.ShapeDtypeStruct((b, lt, DN_VAL_DIM), BF16)
    return pl.pallas_call(
        _delta_body,
        grid=(b, n_blocks, nt),
        in_specs=specs(fwd, 0) + specs(bwd, 1),
        out_specs=[out(fwd), out(bwd)],
        out_shape=[o_shape, o_shape],
        scratch_shapes=[pltpu.VMEM((2, hb, rep, HEAD, HEAD), F32)],
        compiler_params=_cparams("arbitrary", "arbitrary", "arbitrary"),
        name="dn_delta_rule",
    )(qkv, qkv, qkv, gate_rows, qkv, qkv, qkv, gate_rows)


def _outproj_body(*refs, n_o, head_norm, final_norm):
    o_refs = refs[:n_o]
    z_ref, h_ref, mod_ref, w_ref = refs[n_o:n_o + 4]
    rest = list(refs[n_o + 4:])
    og_ref = rest.pop(0) if head_norm else None
    fg_ref = rest.pop(0) if final_norm else None
    out_ref = rest.pop(0)
    d = h_ref.shape[-1]
    width = z_ref.shape[-1]
    parts = []
    for hb in range(width // HEAD):
        cs = slice(hb * HEAD, (hb + 1) * HEAD)
        o = o_refs[0][:, cs].astype(F32)
        for extra in o_refs[1:]:
            o = o + extra[:, cs].astype(F32)
        if head_norm:
            o = o * lax.rsqrt(jnp.mean(o * o, axis=-1, keepdims=True) + NORM_EPS) * og_ref[...]
        parts.append((o * _silu(z_ref[:, cs].astype(F32))).astype(BF16))
    y = jnp.dot(jnp.concatenate(parts, axis=1), w_ref[...], preferred_element_type=F32)
    hn = h_ref[...] + mod_ref[:, 2 * d:3 * d] * y
    if final_norm:
        hn = hn * lax.rsqrt(jnp.mean(hn * hn, axis=-1, keepdims=True) + NORM_EPS) * fg_ref[...]
    out_ref[...] = hn


def _outproj(o_list, z, h, mod4, w_out, layer, head_g=None, final_g=None):
    b, lt, d = h.shape
    nt = lt // ROW_TILE
    d3 = mod4.shape[-1]
    width = z.shape[-1]
    skip = 1 if final_g is not None else 0
    row = lambda c: pl.BlockSpec((None, ROW_TILE, c), lambda bi, r: (bi, r + skip, 0))
    mod_spec = pl.BlockSpec((None, None, 1, d3),
                            lambda bi, r: (layer, jnp.where(r + skip == 0, b, bi), 0, 0))
    in_specs = [row(width)] * len(o_list) + [row(width), row(d), mod_spec, _resident(w_out.shape)]
    args = list(o_list) + [z, h, mod4, w_out]
    if head_g is not None:
        in_specs.append(_resident((1, HEAD)))
        args.append(head_g)
    if final_g is not None:
        in_specs.append(_resident((1, d)))
        args.append(final_g)
    return pl.pallas_call(
        functools.partial(_outproj_body, n_o=len(o_list), head_norm=head_g is not None,
                          final_norm=final_g is not None),
        grid=(b, nt - skip),
        in_specs=in_specs,
        out_specs=pl.BlockSpec((None, ROW_TILE, d), lambda bi, r: (bi, r, 0)),
        out_shape=jax.ShapeDtypeStruct((b, lt - skip * ROW_TILE, d), F32),
        compiler_params=_cparams("arbitrary", "arbitrary"),
        name="mixer_outproj",
    )(*args)


def _rope_partner(x):
    lane = lax.broadcasted_iota(jnp.int32, x.shape, 1)
    quarter = HEAD // 4
    return jnp.where((lane % (2 * quarter)) < quarter,
                     pltpu.roll(x, HEAD - quarter, axis=1), pltpu.roll(x, quarter, axis=1))


def _att_inproj_body(h_ref, mod_ref, g_ref, w_ref, wvt_ref, qg_ref, kg_ref, cos_ref, sin_ref,
                     q_ref, k_ref, vt_ref, z_ref):
    d = h_ref.shape[-1]
    u = _modulated_norm(h_ref[...], g_ref[...], mod_ref[...], d).astype(BF16)
    cos = cos_ref[...]
    sin = sin_ref[...]
    q_scale = (HEAD ** -0.5) * math.log2(math.e)

    def normed_rope(x, g):
        xn = x * lax.rsqrt(jnp.mean(x * x, axis=-1, keepdims=True) + NORM_EPS) * g
        return xn * cos + _rope_partner(xn) * sin

    pair = 2 * HEAD
    for pb in range(ATT_Q_DIM // pair):
        x2 = jnp.dot(u, w_ref[:, pb * pair:(pb + 1) * pair], preferred_element_type=F32)
        for t in range(2):
            cs = slice(pb * pair + t * HEAD, pb * pair + (t + 1) * HEAD)
            q_ref[:, cs] = (normed_rope(x2[:, t * HEAD:(t + 1) * HEAD], qg_ref[...]) * q_scale).astype(BF16)
    for pb in range(ATT_KV_DIM // pair):
        x2 = jnp.dot(u, w_ref[:, ATT_Q_DIM + pb * pair:ATT_Q_DIM + (pb + 1) * pair],
                     preferred_element_type=F32)
        for t in range(2):
            cs = slice(pb * pair + t * HEAD, pb * pair + (t + 1) * HEAD)
            k_ref[:, cs] = normed_rope(x2[:, t * HEAD:(t + 1) * HEAD], kg_ref[...]).astype(BF16)
    vt = lax.dot_general(wvt_ref[...], u, (((1,), (1,)), ((), ())), preferred_element_type=F32)
    for hk in range(ATT_KV_HEADS):
        vt_ref[hk] = vt[hk * HEAD:(hk + 1) * HEAD, :].astype(BF16)
    z0 = ATT_Q_DIM + 2 * ATT_KV_DIM
    for n in range(ATT_Q_DIM // N_COL_CHUNK):
        cs = slice(n * N_COL_CHUNK, (n + 1) * N_COL_CHUNK)
        ws = slice(z0 + n * N_COL_CHUNK, z0 + (n + 1) * N_COL_CHUNK)
        z_ref[:, cs] = jnp.dot(u, w_ref[:, ws], preferred_element_type=F32).astype(BF16)


def _att_inproj(h, mod4, norm_g, w_in, w_vt, q_g, k_g, cos_t, sin_t, layer):
    b, lt, d = h.shape
    nt = lt // ROW_TILE
    d3 = mod4.shape[-1]
    row = lambda c: pl.BlockSpec((None, ROW_TILE, c), lambda bi, r: (bi, r, 0))
    tab = pl.BlockSpec((ROW_TILE, HEAD), lambda bi, r: (r, 0))
    return pl.pallas_call(
        _att_inproj_body,
        grid=(b, nt),
        in_specs=[row(d), _mod_spec(layer, b, d3), _resident((1, d)), _resident(w_in.shape),
                  _resident(w_vt.shape), _resident((1, HEAD)), _resident((1, HEAD)), tab, tab],
        out_specs=[row(ATT_Q_DIM), row(ATT_KV_DIM),
                   pl.BlockSpec((None, ATT_KV_HEADS, None, HEAD, ROW_TILE), lambda bi, r: (bi, 0, r, 0, 0)),
                   row(ATT_Q_DIM)],
        out_shape=[jax.ShapeDtypeStruct((b, lt, ATT_Q_DIM), BF16),
                   jax.ShapeDtypeStruct((b, lt, ATT_KV_DIM), BF16),
                   jax.ShapeDtypeStruct((b, ATT_KV_HEADS, nt, HEAD, ROW_TILE), BF16),
                   jax.ShapeDtypeStruct((b, lt, ATT_Q_DIM), BF16)],
        compiler_params=_cparams("arbitrary", "arbitrary"),
        name="att_inproj",
    )(h, mod4, norm_g, w_in, w_vt, q_g, k_g, cos_t, sin_t)


def _attn_body(q_ref, k_ref, vt_ref, o_ref, s_ref):
    sub = 8
    fold = lambda x, op: op(x.reshape(ROW_TILE // sub, sub, ROW_TILE), axis=0)

    def score_chunk(g, m, chunk):
        st = lax.dot_general(k_ref[chunk * ROW_TILE:(chunk + 1) * ROW_TILE, :],
                             q_ref[:, g * HEAD:(g + 1) * HEAD],
                             (((1,), (1,)), ((), ())), preferred_element_type=F32)
        s_ref[g % 2, chunk] = st
        return jnp.maximum(m, fold(st, jnp.max))

    def value_chunk(g, m_row, l, acc, chunk):
        pt = jnp.exp2(s_ref[g % 2, chunk] - m_row)
        l = l + fold(pt, jnp.sum)
        return l, acc + jnp.dot(vt_ref[chunk], pt.astype(BF16), preferred_element_type=F32)

    def phase(n_chunks, g_score, g_value, m_row):
        m = jnp.full((sub, ROW_TILE), NEG_BIG, F32)
        l = jnp.zeros((sub, ROW_TILE), F32)
        acc = jnp.zeros((HEAD, ROW_TILE), F32)
        for chunk in range(n_chunks):
            if g_score is not None:
                m = score_chunk(g_score, m, chunk)
            if g_value is not None:
                l, acc = value_chunk(g_value, m_row, l, acc, chunk)
        if g_value is not None:
            o_t = acc / jnp.sum(l, axis=0, keepdims=True)
            o_ref[:, g_value * HEAD:(g_value + 1) * HEAD] = o_t.T.astype(BF16)
        return jnp.max(m, axis=0, keepdims=True) if g_score is not None else None

    def attend(n_chunks):
        m_row = phase(n_chunks, 0, None, None)
        for g in range(ATT_GROUP):
            m_row = phase(n_chunks, g + 1 if g + 1 < ATT_GROUP else None, g, m_row)

    is_ctx = pl.program_id(2) == 0
    pl.when(is_ctx)(functools.partial(attend, 1))
    pl.when(jnp.logical_not(is_ctx))(functools.partial(attend, k_ref.shape[0] // ROW_TILE))


def _attention(q, k, vt):
    b, lt, _ = q.shape
    nt = lt // ROW_TILE
    gw = ATT_GROUP * HEAD
    return pl.pallas_call(
        _attn_body,
        grid=(b, ATT_KV_HEADS, nt),
        in_specs=[
            pl.BlockSpec((None, ROW_TILE, gw), lambda bi, hk, r: (bi, r, hk)),
            pl.BlockSpec((None, lt, HEAD), lambda bi, hk, r: (bi, 0, hk)),
            pl.BlockSpec((None, None, nt, HEAD, ROW_TILE), lambda bi, hk, r: (bi, hk, 0, 0, 0)),
        ],
        out_specs=pl.BlockSpec((None, ROW_TILE, gw), lambda bi, hk, r: (bi, r, hk)),
        out_shape=jax.ShapeDtypeStruct((b, lt, ATT_Q_DIM), BF16),
        scratch_shapes=[pltpu.VMEM((2, nt, ROW_TILE, ROW_TILE), F32)],
        compiler_params=_cparams("arbitrary", "arbitrary", "arbitrary"),
        name="gqa_attention",
    )(q, k, vt)


def _rope_tables(n_lat, n_ctx):
    t = jnp.arange(n_lat)
    row = (t // GRID_W).astype(F32)
    col = (t % GRID_W).astype(F32)
    axis_dim = HEAD // 2
    inv = ROPE_THETA ** (-jnp.arange(0, axis_dim, 2, dtype=F32) / axis_dim)
    ang_r = row[:, None] * inv
    ang_c = col[:, None] * inv
    cos = jnp.concatenate([jnp.cos(ang_r)] * 2 + [jnp.cos(ang_c)] * 2, axis=-1)
    sin = jnp.concatenate([-jnp.sin(ang_r), jnp.sin(ang_r), -jnp.sin(ang_c), jnp.sin(ang_c)], axis=-1)
    cos = jnp.concatenate([jnp.ones((n_ctx, HEAD), F32), cos], axis=0)
    sin = jnp.concatenate([jnp.zeros((n_ctx, HEAD), F32), sin], axis=0)
    return cos, sin


def _lane_pad(vec, offset):
    out = jnp.zeros((1, LANES), F32)
    return lax.dynamic_update_slice(out, vec.reshape(1, -1).astype(F32), (0, offset))


def kernel(x, c, ctx, c_ctx, norm_g, ada_w, ada_b, dn_w_in, dn_conv_w, dn_a_log, dn_dt_bias,
           dn_o_norm_g, dn_w_out, att_w_in, att_q_norm_g, att_k_norm_g, att_w_out, final_norm_g):
    b, n_lat, d = x.shape
    n_ctx = ctx.shape[1]
    depth = norm_g.shape[0]
    assert n_ctx == ROW_TILE and n_lat % ROW_TILE == 0 and b < MOD_ROWS and d % LANES == 0
    lt = n_ctx + n_lat
    rep = DN_V_HEADS // DN_QK_HEADS

    cvec = jnp.concatenate([c, c_ctx[None, :], jnp.zeros((MOD_ROWS - b - 1, d), F32)], axis=0)
    mod4 = _modulation(cvec, ada_w, ada_b).reshape(depth, MOD_ROWS, 1, 3 * d)
    h = jnp.concatenate([ctx, x], axis=1)
    cos_t, sin_t = _rope_tables(n_lat, n_ctx)

    for i in range(depth):
        j = i // 2
        last = i == depth - 1
        g_i = norm_g[i].reshape(1, d)
        if i % 2 == 0:
            w_in = dn_w_in[j].astype(BF16)
            w_main = w_in[:, :DN_CONV_DIM + DN_VAL_DIM]
            w_ab = jnp.pad(w_in[:, DN_CONV_DIM + DN_VAL_DIM:], ((0, 0), (0, LANES - 4 * DN_V_HEADS)))
            qkv_pre, z, ab = _dn_inproj(h, mod4, g_i, w_main, w_ab, i)
            alog_vec = _lane_pad(dn_a_log[j], 2 * DN_V_HEADS)
            dtb_vec = _lane_pad(dn_dt_bias[j], 2 * DN_V_HEADS)
            qkv, gates = _dn_conv(qkv_pre, ab, dn_conv_w[j], alog_vec, dtb_vec)
            gt = gates[:, :, :4 * DN_V_HEADS].reshape(b, lt, 2, 2, DN_QK_HEADS, rep)
            gate_rows = jnp.transpose(gt[:, :, ::-1], (0, 3, 4, 2, 5, 1)).reshape(
                b, 2, DN_QK_HEADS, 2 * rep, lt)
            o_f, o_b = _delta_rule(qkv, gate_rows)
            h = _outproj([o_f, o_b], z, h, mod4, dn_w_out[j].astype(BF16), i,
                         head_g=dn_o_norm_g[j].reshape(1, HEAD),
                         final_g=final_norm_g.reshape(1, d) if last else None)
        else:
            w_in = att_w_in[j].astype(BF16)
            w_vt = w_in[:, ATT_Q_DIM + ATT_KV_DIM:ATT_Q_DIM + 2 * ATT_KV_DIM].T
            q, k, vt, z = _att_inproj(h, mod4, g_i, w_in, w_vt,
                                      att_q_norm_g[j].reshape(1, HEAD), att_k_norm_g[j].reshape(1, HEAD),
                                      cos_t, sin_t, i)
            o = _attention(q, k, vt)
            h = _outproj([o], z, h, mod4, att_w_out[j].astype(BF16), i,
                         final_g=final_norm_g.reshape(1, d) if last else None)
    return h
```

```python
import functools
import math

import jax
import jax.numpy as jnp
from jax import lax
from jax.experimental import pallas as pl
from jax.experimental.pallas import tpu as pltpu

F32 = jnp.float32
BF16 = jnp.bfloat16

NORM_EPS = 1e-6
GRID_W = 64
ROPE_THETA = 10000.0

HEAD = 128
DN_QK_HEADS = 8
DN_V_HEADS = 16
DN_KEY_DIM = DN_QK_HEADS * HEAD
DN_VAL_DIM = DN_V_HEADS * HEAD
DN_CONV_DIM = 2 * DN_KEY_DIM + DN_VAL_DIM
DN_CONV_K = 5
DN_CHUNK = 64
ATT_Q_HEADS = 8
ATT_KV_HEADS = 2
ATT_GROUP = ATT_Q_HEADS // ATT_KV_HEADS
ATT_Q_DIM = ATT_Q_HEADS * HEAD
ATT_KV_DIM = ATT_KV_HEADS * HEAD

ROW_TILE = 256
LANES = 128
HALO = 8
MOD_ROWS = 16
N_COL_CHUNK = 512
DN_HEAD_BLOCK = 4
VMEM_LIMIT = 56 * 1024 * 1024
NEG_BIG = -1e30


def _cparams(*sem):
    return pltpu.CompilerParams(dimension_semantics=sem, vmem_limit_bytes=VMEM_LIMIT)


def _resident(shape):
    return pl.BlockSpec(shape, lambda *_: (0,) * len(shape), pipeline_mode=pl.Buffered(1))


def _silu(x):
    hx = 0.5 * x
    return hx + hx * jnp.tanh(hx)


def _mod_body(c_ref, w_ref, b_ref, o_ref):
    sc = _silu(c_ref[...])
    o_ref[...] = jnp.dot(sc, w_ref[...], preferred_element_type=F32) + b_ref[...]


def _modulation(cvec, ada_w, ada_b):
    depth, d, d3 = ada_w.shape
    tn = 1024
    return pl.pallas_call(
        _mod_body,
        grid=(depth, d3 // tn),
        in_specs=[
            pl.BlockSpec((MOD_ROWS, d), lambda i, j: (0, 0)),
            pl.BlockSpec((None, d, tn), lambda i, j: (i, 0, j)),
            pl.BlockSpec((None, 1, tn), lambda i, j: (i, 0, j)),
        ],
        out_specs=pl.BlockSpec((None, MOD_ROWS, tn), lambda i, j: (i, 0, j)),
        out_shape=jax.ShapeDtypeStruct((depth, MOD_ROWS, d3), F32),
        compiler_params=_cparams("arbitrary", "arbitrary"),
        name="adaln_modulation",
    )(cvec, ada_w, ada_b.reshape(depth, 1, d3))


def _mod_spec(layer, batch, d3):
    return pl.BlockSpec((None, None, 1, d3),
                        lambda b, r: (layer, jnp.where(r == 0, batch, b), 0, 0))


def _modulated_norm(x, g, mod, d):
    ms = jnp.mean(x * x, axis=-1, keepdims=True)
    xn = x * lax.rsqrt(ms + NORM_EPS) * g
    return xn * (1.0 + mod[:, d:2 * d]) + mod[:, 0:d]


def _dn_inproj_body(h_ref, mod_ref, g_ref, w_ref, wab_ref, qkv_ref, z_ref, ab_ref):
    d = h_ref.shape[-1]
    u = _modulated_norm(h_ref[...], g_ref[...], mod_ref[...], d).astype(BF16)
    for n in range(DN_CONV_DIM // N_COL_CHUNK):
        cs = slice(n * N_COL_CHUNK, (n + 1) * N_COL_CHUNK)
        qkv_ref[:, cs] = jnp.dot(u, w_ref[:, cs], preferred_element_type=F32)
    for n in range(DN_VAL_DIM // N_COL_CHUNK):
        cs = slice(n * N_COL_CHUNK, (n + 1) * N_COL_CHUNK)
        ws = slice(DN_CONV_DIM + n * N_COL_CHUNK, DN_CONV_DIM + (n + 1) * N_COL_CHUNK)
        z_ref[:, cs] = jnp.dot(u, w_ref[:, ws], preferred_element_type=F32).astype(BF16)
    ab_ref[...] = jnp.dot(u, wab_ref[...], preferred_element_type=F32)


def _dn_inproj(h, mod4, norm_g, w_main, w_ab, layer):
    b, lt, d = h.shape
    nt = lt // ROW_TILE
    d3 = mod4.shape[-1]
    row = lambda c: pl.BlockSpec((None, ROW_TILE, c), lambda bi, r: (bi, r, 0))
    return pl.pallas_call(
        _dn_inproj_body,
        grid=(b, nt),
        in_specs=[row(d), _mod_spec(layer, b, d3), _resident((1, d)),
                  _resident(w_main.shape), _resident(w_ab.shape)],
        out_specs=[row(DN_CONV_DIM), row(DN_VAL_DIM), row(LANES)],
        out_shape=[jax.ShapeDtypeStruct((b, lt, DN_CONV_DIM), F32),
                   jax.ShapeDtypeStruct((b, lt, DN_VAL_DIM), BF16),
                   jax.ShapeDtypeStruct((b, lt, LANES), F32)],
        compiler_params=_cparams("arbitrary", "arbitrary"),
        name="dn_inproj",
    )(h, mod4, norm_g, w_main, w_ab)


def _dn_conv_body(x_ref, prev_ref, next_ref, cw_ref, ab_ref, alog_ref, dtb_ref,
                  qkv_ref, gate_ref, pad_ref):
    r = pl.program_id(1)
    nt = pl.num_programs(1)
    has_prev = r > 1
    has_next = jnp.logical_and(r > 0, r < nt - 1)
    half = DN_CONV_K // 2
    for hb in range(DN_CONV_DIM // HEAD):
        cs = slice(hb * HEAD, (hb + 1) * HEAD)
        pad_ref[0:HALO, :] = jnp.where(has_prev, prev_ref[:, cs], 0.0)
        pad_ref[HALO:HALO + ROW_TILE, :] = x_ref[:, cs]
        pad_ref[HALO + ROW_TILE:, :] = jnp.where(has_next, next_ref[:, cs], 0.0)
        acc = None
        for j in range(DN_CONV_K):
            win = pad_ref[pl.ds(HALO - half + j, ROW_TILE), :]
            term = win * cw_ref[j:j + 1, cs]
            acc = term if acc is None else acc + term
        y = _silu(acc)
        if hb < 2 * DN_QK_HEADS:
            y = y * lax.rsqrt(jnp.sum(y * y, axis=-1, keepdims=True) + NORM_EPS)
        qkv_ref[:, cs] = y.astype(BF16)
    ab = ab_ref[...]
    lane = lax.broadcasted_iota(jnp.int32, ab.shape, 1)
    beta = jax.nn.sigmoid(ab)
    g = -jnp.exp(alog_ref[...]) * jax.nn.softplus(ab + dtb_ref[...])
    ti = lax.broadcasted_iota(jnp.int32, (ROW_TILE, ROW_TILE), 0)
    tj = lax.broadcasted_iota(jnp.int32, (ROW_TILE, ROW_TILE), 1)
    prefix_mat = jnp.where(jnp.logical_and(ti // DN_CHUNK == tj // DN_CHUNK, tj <= ti), 1.0, 0.0)
    prefix = jnp.dot(prefix_mat, g, precision=lax.Precision.HIGHEST, preferred_element_type=F32)
    row_chunk = lax.broadcasted_iota(jnp.int32, ab.shape, 0) // DN_CHUNK
    total = jnp.zeros_like(prefix)
    for ch in range(ROW_TILE // DN_CHUNK):
        end = (ch + 1) * DN_CHUNK
        total = jnp.where(row_chunk == ch, prefix[end - 1:end, :], total)
    backward_lane = lane >= 3 * DN_V_HEADS
    gcum = jnp.where(backward_lane, total - prefix + g, prefix)
    gate_ref[...] = jnp.where(lane < 2 * DN_V_HEADS, beta, gcum)


def _dn_conv(qkv_pre, ab, conv_w, alog_vec, dtb_vec):
    b, lt, c = qkv_pre.shape
    nt = lt // ROW_TILE
    per = ROW_TILE // HALO
    nhalo = lt // HALO
    row = lambda w: pl.BlockSpec((None, ROW_TILE, w), lambda bi, r: (bi, r, 0))
    return pl.pallas_call(
        _dn_conv_body,
        grid=(b, nt),
        in_specs=[
            row(c),
            pl.BlockSpec((None, HALO, c), lambda bi, r: (bi, jnp.maximum(r * per - 1, 0), 0)),
            pl.BlockSpec((None, HALO, c), lambda bi, r: (bi, jnp.minimum((r + 1) * per, nhalo - 1), 0)),
            _resident(conv_w.shape), row(LANES), _resident((1, LANES)), _resident((1, LANES)),
        ],
        out_specs=[row(c), row(LANES)],
        out_shape=[jax.ShapeDtypeStruct((b, lt, c), BF16), jax.ShapeDtypeStruct((b, lt, LANES), F32)],
        scratch_shapes=[pltpu.VMEM((ROW_TILE + 2 * HALO, HEAD), F32)],
        compiler_params=_cparams("arbitrary", "arbitrary"),
        name="dn_conv_gates",
    )(qkv_pre, qkv_pre, qkv_pre, conv_w, ab, alog_vec, dtb_vec)


def _dot(a, b):
    return jnp.dot(a.astype(BF16), b.astype(BF16), preferred_element_type=F32)


def _dot_nt(a, b):
    return lax.dot_general(a.astype(BF16), b.astype(BF16), (((1,), (1,)), ((), ())),
                           preferred_element_type=F32)


def _dot_tn(a, b):
    return lax.dot_general(a.astype(BF16), b.astype(BF16), (((0,), (0,)), ((), ())),
                           preferred_element_type=F32)


def _pair_diag(x):
    lane = lax.broadcasted_iota(jnp.int32, x.shape, 1)
    zero = jnp.zeros_like(x)
    c = x.shape[0]
    return jnp.concatenate([jnp.where(lane < c, x, zero), jnp.where(lane >= c, x, zero)], axis=0)


def _pair_matmul(a, x):
    return jnp.dot(a.astype(BF16), _pair_diag(x.astype(BF16)), preferred_element_type=F32)


def _delta_body(qf_ref, kf_ref, vf_ref, gf_ref, qb_ref, kb_ref, vb_ref, gb_ref,
                of_ref, ob_ref, s_ref):
    c = DN_CHUNK
    n_chunks = ROW_TILE // c
    rep = DN_V_HEADS // DN_QK_HEADS
    assert rep == 2 and 2 * c == LANES
    q_scale = HEAD ** -0.5

    @pl.when(pl.program_id(2) == 0)
    def _():
        s_ref[...] = jnp.zeros_like(s_ref)

    ci = lax.broadcasted_iota(jnp.int32, (c, LANES), 0)
    cl = lax.broadcasted_iota(jnp.int32, (c, LANES), 1)
    cj = cl % c
    head1 = cl >= c
    eye_p = jnp.where(ci == cj, 1.0, 0.0)

    walks = []
    for reverse, (q_ref, k_ref, v_ref, g_ref, o_ref) in enumerate(
            ((qf_ref, kf_ref, vf_ref, gf_ref, of_ref), (qb_ref, kb_ref, vb_ref, gb_ref, ob_ref))):
        for hb in range(DN_HEAD_BLOCK):
            gr = g_ref[hb]
            gr8 = jnp.concatenate([gr, jnp.zeros((8 - gr.shape[0], ROW_TILE), F32)], axis=0)
            walks.append(dict(
                reverse=reverse, hb=hb, q_ref=q_ref, k_ref=k_ref, v_ref=v_ref, o_ref=o_ref,
                qk_cols=slice(hb * HEAD, (hb + 1) * HEAD), v_col0=hb * rep * HEAD,
                row=gr8, col=gr8.T,
                incl=(cj >= ci) if reverse else (cj <= ci),
                strict=(cj > ci) if reverse else (cj < ci),
                order=list(range(n_chunks - 1, -1, -1) if reverse else range(n_chunks))))

    def prepare(pairs):
        items = []
        for w, ch in pairs:
            rows = slice(ch * c, (ch + 1) * c)
            gcol = [w["col"][rows, s:s + 1] for s in range(rep)]
            bcol = [w["col"][rows, rep + s:rep + s + 1] for s in range(rep)]
            last = 0 if w["reverse"] else c - 1
            items.append(dict(w=w, rows=rows, gcol=gcol, bcol=bcol,
                              gtot=[g[last:last + 1, :] for g in gcol]))
        for it in items:
            w, rows = it["w"], it["rows"]
            kc = w["k_ref"][rows, w["qk_cols"]]
            qc = w["q_ref"][rows, w["qk_cols"]]
            k2 = jnp.concatenate([kc, kc], axis=0)
            it.update(kf=kc.astype(F32), qf=qc.astype(F32), kk=_dot_nt(kc, k2), qk=_dot_nt(qc, k2))
        for it in items:
            w = it["w"]
            grow_p = jnp.concatenate([w["row"][s:s + 1, it["rows"]] for s in range(rep)], axis=1)
            gcol_p = jnp.where(head1, it["gcol"][1], it["gcol"][0])
            it["beta_p"] = jnp.where(head1, it["bcol"][1], it["bcol"][0])
            it["decay_p"] = jnp.exp(jnp.where(w["incl"], gcol_p - grow_p, NEG_BIG))
        for it in items:
            it["a"] = jnp.where(it["w"]["strict"], it["beta_p"] * it.pop("kk") * it["decay_p"], 0.0)
            it["qkm"] = q_scale * it.pop("qk") * it.pop("decay_p")
        return items

    def invert(items):
        for it in items:
            it["x"] = _pair_matmul(it["a"], it["a"])
            it["t"] = eye_p - it["a"]
        for _ in range(int(math.log2(c)) - 3):
            for it in items:
                r2 = _pair_matmul(jnp.concatenate([it["x"], it["t"]], axis=0), it["x"])
                it["x"] = r2[:c]
                it["t"] = it["t"] + r2[c:]
        for it in items:
            it["t"] = it["t"] + _pair_matmul(it["t"], it["x"])
        for it in items:
            it["res"] = eye_p - it["t"] - _pair_matmul(it["a"], it["t"])
        for it in items:
            it["t"] = it["t"] + _pair_matmul(it["t"], it.pop("res"))

    def chunk_products(items):
        for it in items:
            rhs = []
            for s in range(rep):
                eg = jnp.exp(it["gcol"][s])
                v0 = it["w"]["v_col0"] + s * HEAD
                vf = it["w"]["v_ref"][it["rows"], v0:v0 + HEAD].astype(F32)
                rhs.append(jnp.concatenate([vf * it["bcol"][s], it["kf"] * (it["bcol"][s] * eg)], axis=1))
            it["rhs"] = jnp.concatenate(rhs, axis=0).astype(BF16)
            it["k_dec"] = [(it["kf"] * jnp.exp(it["gtot"][s] - it["gcol"][s])).astype(BF16) for s in range(rep)]
            it["qd"] = [it["qf"] * (q_scale * jnp.exp(it["gcol"][s])) for s in range(rep)]
            it["e_tot"] = [jnp.exp(it["gtot"][s]) for s in range(rep)]
        for it in items:
            it["uw"] = _dot(_pair_diag(it["t"]), it.pop("rhs"))
        for it in items:
            it["qo"] = _dot(_pair_diag(it["qkm"]), it["uw"])
        for it in items:
            it["gn"] = [_dot_tn(it["k_dec"][s], it["uw"][s * c:(s + 1) * c])
                        for s in range(rep)]
        for it in items:
            it["q_eff"] = [it["qd"][s] - it["qo"][s * c:(s + 1) * c, HEAD:] for s in range(rep)]

    def advance(items):
        for it in items:
            state = it["w"]["state"]
            for s in range(rep):
                rs = _dot(jnp.concatenate([it["gn"][s][:, HEAD:], it["q_eff"][s]], axis=0), state[s])
                state[s] = state[s] * it["e_tot"][s] - rs[:HEAD] + it["gn"][s][:, :HEAD]
                v0 = it["w"]["v_col0"] + s * HEAD
                it["w"]["o_ref"][it["rows"], v0:v0 + HEAD] = (
                    rs[HEAD:] + it["qo"][s * c:(s + 1) * c, :HEAD]).astype(BF16)

    for w in walks:
        w["state"] = [s_ref[w["reverse"], w["hb"], s] for s in range(rep)]
    items = prepare([(w, w["order"][rank]) for rank in range(n_chunks) for w in walks])
    invert(items)
    chunk_products(items)
    for rank in range(n_chunks):
        advance(items[rank * len(walks):(rank + 1) * len(walks)])
    for w in walks:
        for s in range(rep):
            s_ref[w["reverse"], w["hb"], s] = w["state"][s]


def _delta_rule(qkv, gate_rows):
    b, lt, _ = qkv.shape
    nt = lt // ROW_TILE
    rep = DN_V_HEADS // DN_QK_HEADS
    v_col0 = 2 * DN_KEY_DIM // (DN_HEAD_BLOCK * rep * HEAD)
    fwd = lambda r: r
    bwd = lambda r: jnp.where(r == 0, 0, nt - r)

    hb = DN_HEAD_BLOCK
    n_blocks = DN_QK_HEADS // hb

    def specs(tile, d):
        return [
            pl.BlockSpec((None, ROW_TILE, hb * HEAD), lambda bi, j, r: (bi, tile(r), j)),
            pl.BlockSpec((None, ROW_TILE, hb * HEAD), lambda bi, j, r: (bi, tile(r), n_blocks + j)),
            pl.BlockSpec((None, ROW_TILE, hb * rep * HEAD), lambda bi, j, r: (bi, tile(r), v_col0 + j)),
            pl.BlockSpec((None, None, hb, 2 * rep, ROW_TILE), lambda bi, j, r: (bi, d, j, 0, tile(r))),
        ]

    out = lambda tile: pl.BlockSpec((None, ROW_TILE, hb * rep * HEAD), lambda bi, j, r: (bi, tile(r), j))
    o_shape = jax.ShapeDtypeStruct((b, lt, DN_VAL_DIM), BF16)
    return pl.pallas_call(
        _delta_body,
        grid=(b, n_blocks, nt),
        in_specs=specs(fwd, 0) + specs(bwd, 1),
        out_specs=[out(fwd), out(bwd)],
        out_shape=[o_shape, o_shape],
        scratch_shapes=[pltpu.VMEM((2, hb, rep, HEAD, HEAD), F32)],
        compiler_params=_cparams("arbitrary", "arbitrary", "arbitrary"),
        name="dn_delta_rule",
    )(qkv, qkv, qkv, gate_rows, qkv, qkv, qkv, gate_rows)


def _outproj_body(*refs, n_o, head_norm, final_norm):
    o_refs = refs[:n_o]
    z_ref, h_ref, mod_ref, w_ref = refs[n_o:n_o + 4]
    rest = list(refs[n_o + 4:])
    og_ref = rest.pop(0) if head_norm else None
    fg_ref = rest.pop(0) if final_norm else None
    out_ref = rest.pop(0)
    d = h_ref.shape[-1]
    width = z_ref.shape[-1]
    parts = []
    for hb in range(width // HEAD):
        cs = slice(hb * HEAD, (hb + 1) * HEAD)
        o = o_refs[0][:, cs].astype(F32)
        for extra in o_refs[1:]:
            o = o + extra[:, cs].astype(F32)
        if head_norm:
            o = o * lax.rsqrt(jnp.mean(o * o, axis=-1, keepdims=True) + NORM_EPS) * og_ref[...]
        parts.append((o * _silu(z_ref[:, cs].astype(F32))).astype(BF16))
    y = jnp.dot(jnp.concatenate(parts, axis=1), w_ref[...], preferred_element_type=F32)
    hn = h_ref[...] + mod_ref[:, 2 * d:3 * d] * y
    if final_norm:
        hn = hn * lax.rsqrt(jnp.mean(hn * hn, axis=-1, keepdims=True) + NORM_EPS) * fg_ref[...]
    out_ref[...] = hn


def _outproj(o_list, z, h, mod4, w_out, layer, head_g=None, final_g=None):
    b, lt, d = h.shape
    nt = lt // ROW_TILE
    d3 = mod4.shape[-1]
    width = z.shape[-1]
    skip = 1 if final_g is not None else 0
    row = lambda c: pl.BlockSpec((None, ROW_TILE, c), lambda bi, r: (bi, r + skip, 0))
    mod_spec = pl.BlockSpec((None, None, 1, d3),
                            lambda bi, r: (layer, jnp.where(r + skip == 0, b, bi), 0, 0))
    in_specs = [row(width)] * len(o_list) + [row(width), row(d), mod_spec, _resident(w_out.shape)]
    args = list(o_list) + [z, h, mod4, w_out]
    if head_g is not None:
        in_specs.append(_resident((1, HEAD)))
        args.append(head_g)
    if final_g is not None:
        in_specs.append(_resident((1, d)))
        args.append(final_g)
    return pl.pallas_call(
        functools.partial(_outproj_body, n_o=len(o_list), head_norm=head_g is not None,
                          final_norm=final_g is not None),
        grid=(b, nt - skip),
        in_specs=in_specs,
        out_specs=pl.BlockSpec((None, ROW_TILE, d), lambda bi, r: (bi, r, 0)),
        out_shape=jax.ShapeDtypeStruct((b, lt - skip * ROW_TILE, d), F32),
        compiler_params=_cparams("arbitrary", "arbitrary"),
        name="mixer_outproj",
    )(*args)


def _rope_partner(x):
    lane = lax.broadcasted_iota(jnp.int32, x.shape, 1)
    quarter = HEAD // 4
    return jnp.where((lane % (2 * quarter)) < quarter,
                     pltpu.roll(x, HEAD - quarter, axis=1), pltpu.roll(x, quarter, axis=1))


def _att_inproj_body(h_ref, mod_ref, g_ref, w_ref, wvt_ref, qg_ref, kg_ref, cos_ref, sin_ref,
                     q_ref, k_ref, vt_ref, z_ref):
    d = h_ref.shape[-1]
    u = _modulated_norm(h_ref[...], g_ref[...], mod_ref[...], d).astype(BF16)
    cos = cos_ref[...]
    sin = sin_ref[...]
    q_scale = (HEAD ** -0.5) * math.log2(math.e)

    def normed_rope(x, g):
        xn = x * lax.rsqrt(jnp.mean(x * x, axis=-1, keepdims=True) + NORM_EPS) * g
        return xn * cos + _rope_partner(xn) * sin

    pair = 2 * HEAD
    for pb in range(ATT_Q_DIM // pair):
        x2 = jnp.dot(u, w_ref[:, pb * pair:(pb + 1) * pair], preferred_element_type=F32)
        for t in range(2):
            cs = slice(pb * pair + t * HEAD, pb * pair + (t + 1) * HEAD)
            q_ref[:, cs] = (normed_rope(x2[:, t * HEAD:(t + 1) * HEAD], qg_ref[...]) * q_scale).astype(BF16)
    for pb in range(ATT_KV_DIM // pair):
        x2 = jnp.dot(u, w_ref[:, ATT_Q_DIM + pb * pair:ATT_Q_DIM + (pb + 1) * pair],
                     preferred_element_type=F32)
        for t in range(2):
            cs = slice(pb * pair + t * HEAD, pb * pair + (t + 1) * HEAD)
            k_ref[:, cs] = normed_rope(x2[:, t * HEAD:(t + 1) * HEAD], kg_ref[...]).astype(BF16)
    vt = lax.dot_general(wvt_ref[...], u, (((1,), (1,)), ((), ())), preferred_element_type=F32)
    for hk in range(ATT_KV_HEADS):
        vt_ref[hk] = vt[hk * HEAD:(hk + 1) * HEAD, :].astype(BF16)
    z0 = ATT_Q_DIM + 2 * ATT_KV_DIM
    for n in range(ATT_Q_DIM // N_COL_CHUNK):
        cs = slice(n * N_COL_CHUNK, (n + 1) * N_COL_CHUNK)
        ws = slice(z0 + n * N_COL_CHUNK, z0 + (n + 1) * N_COL_CHUNK)
        z_ref[:, cs] = jnp.dot(u, w_ref[:, ws], preferred_element_type=F32).astype(BF16)


def _att_inproj(h, mod4, norm_g, w_in, w_vt, q_g, k_g, cos_t, sin_t, layer):
    b, lt, d = h.shape
    nt = lt // ROW_TILE
    d3 = mod4.shape[-1]
    row = lambda c: pl.BlockSpec((None, ROW_TILE, c), lambda bi, r: (bi, r, 0))
    tab = pl.BlockSpec((ROW_TILE, HEAD), lambda bi, r: (r, 0))
    return pl.pallas_call(
        _att_inproj_body,
        grid=(b, nt),
        in_specs=[row(d), _mod_spec(layer, b, d3), _resident((1, d)), _resident(w_in.shape),
                  _resident(w_vt.shape), _resident((1, HEAD)), _resident((1, HEAD)), tab, tab],
        out_specs=[row(ATT_Q_DIM), row(ATT_KV_DIM),
                   pl.BlockSpec((None, ATT_KV_HEADS, None, HEAD, ROW_TILE), lambda bi, r: (bi, 0, r, 0, 0)),
                   row(ATT_Q_DIM)],
        out_shape=[jax.ShapeDtypeStruct((b, lt, ATT_Q_DIM), BF16),
                   jax.ShapeDtypeStruct((b, lt, ATT_KV_DIM), BF16),
                   jax.ShapeDtypeStruct((b, ATT_KV_HEADS, nt, HEAD, ROW_TILE), BF16),
                   jax.ShapeDtypeStruct((b, lt, ATT_Q_DIM), BF16)],
        compiler_params=_cparams("arbitrary", "arbitrary"),
        name="att_inproj",
    )(h, mod4, norm_g, w_in, w_vt, q_g, k_g, cos_t, sin_t)


def _attn_body(q_ref, k_ref, vt_ref, o_ref, s_ref):
    sub = 8
    fold = lambda x, op: op(x.reshape(ROW_TILE // sub, sub, ROW_TILE), axis=0)

    def score_chunk(g, m, chunk):
        st = lax.dot_general(k_ref[chunk * ROW_TILE:(chunk + 1) * ROW_TILE, :],
                             q_ref[:, g * HEAD:(g + 1) * HEAD],
                             (((1,), (1,)), ((), ())), preferred_element_type=F32)
        s_ref[g % 2, chunk] = st
        return jnp.maximum(m, fold(st, jnp.max))

    def value_chunk(g, m_row, l, acc, chunk):
        pt = jnp.exp2(s_ref[g % 2, chunk] - m_row)
        l = l + fold(pt, jnp.sum)
        return l, acc + jnp.dot(vt_ref[chunk], pt.astype(BF16), preferred_element_type=F32)

    def phase(n_chunks, g_score, g_value, m_row):
        m = jnp.full((sub, ROW_TILE), NEG_BIG, F32)
        l = jnp.zeros((sub, ROW_TILE), F32)
        acc = jnp.zeros((HEAD, ROW_TILE), F32)
        for chunk in range(n_chunks):
            if g_score is not None:
                m = score_chunk(g_score, m, chunk)
            if g_value is not None:
                l, acc = value_chunk(g_value, m_row, l, acc, chunk)
        if g_value is not None:
            o_t = acc / jnp.sum(l, axis=0, keepdims=True)
            o_ref[:, g_value * HEAD:(g_value + 1) * HEAD] = o_t.T.astype(BF16)
        return jnp.max(m, axis=0, keepdims=True) if g_score is not None else None

    def attend(n_chunks):
        m_row = phase(n_chunks, 0, None, None)
        for g in range(ATT_GROUP):
            m_row = phase(n_chunks, g + 1 if g + 1 < ATT_GROUP else None, g, m_row)

    is_ctx = pl.program_id(2) == 0
    pl.when(is_ctx)(functools.partial(attend, 1))
    pl.when(jnp.logical_not(is_ctx))(functools.partial(attend, k_ref.shape[0] // ROW_TILE))


def _attention(q, k, vt):
    b, lt, _ = q.shape
    nt = lt // ROW_TILE
    gw = ATT_GROUP * HEAD
    return pl.pallas_call(
        _attn_body,
        grid=(b, ATT_KV_HEADS, nt),
        in_specs=[
            pl.BlockSpec((None, ROW_TILE, gw), lambda bi, hk, r: (bi, r, hk)),
            pl.BlockSpec((None, lt, HEAD), lambda bi, hk, r: (bi, 0, hk)),
            pl.BlockSpec((None, None, nt, HEAD, ROW_TILE), lambda bi, hk, r: (bi, hk, 0, 0, 0)),
        ],
        out_specs=pl.BlockSpec((None, ROW_TILE, gw), lambda bi, hk, r: (bi, r, hk)),
        out_shape=jax.ShapeDtypeStruct((b, lt, ATT_Q_DIM), BF16),
        scratch_shapes=[pltpu.VMEM((2, nt, ROW_TILE, ROW_TILE), F32)],
        compiler_params=_cparams("arbitrary", "arbitrary", "arbitrary"),
        name="gqa_attention",
    )(q, k, vt)


def _rope_tables(n_lat, n_ctx):
    t = jnp.arange(n_lat)
    row = (t // GRID_W).astype(F32)
    col = (t % GRID_W).astype(F32)
    axis_dim = HEAD // 2
    inv = ROPE_THETA ** (-jnp.arange(0, axis_dim, 2, dtype=F32) / axis_dim)
    ang_r = row[:, None] * inv
    ang_c = col[:, None] * inv
    cos = jnp.concatenate([jnp.cos(ang_r)] * 2 + [jnp.cos(ang_c)] * 2, axis=-1)
    sin = jnp.concatenate([-jnp.sin(ang_r), jnp.sin(ang_r), -jnp.sin(ang_c), jnp.sin(ang_c)], axis=-1)
    cos = jnp.concatenate([jnp.ones((n_ctx, HEAD), F32), cos], axis=0)
    sin = jnp.concatenate([jnp.zeros((n_ctx, HEAD), F32), sin], axis=0)
    return cos, sin


def _lane_pad(vec, offset):
    out = jnp.zeros((1, LANES), F32)
    return lax.dynamic_update_slice(out, vec.reshape(1, -1).astype(F32), (0, offset))


def kernel(x, c, ctx, c_ctx, norm_g, ada_w, ada_b, dn_w_in, dn_conv_w, dn_a_log, dn_dt_bias,
           dn_o_norm_g, dn_w_out, att_w_in, att_q_norm_g, att_k_norm_g, att_w_out, final_norm_g):
    b, n_lat, d = x.shape
    n_ctx = ctx.shape[1]
    depth = norm_g.shape[0]
    assert n_ctx == ROW_TILE and n_lat % ROW_TILE == 0 and b < MOD_ROWS and d % LANES == 0
    lt = n_ctx + n_lat
    rep = DN_V_HEADS // DN_QK_HEADS

    cvec = jnp.concatenate([c, c_ctx[None, :], jnp.zeros((MOD_ROWS - b - 1, d), F32)], axis=0)
    mod4 = _modulation(cvec, ada_w, ada_b).reshape(depth, MOD_ROWS, 1, 3 * d)
    h = jnp.concatenate([ctx, x], axis=1)
    cos_t, sin_t = _rope_tables(n_lat, n_ctx)

    for i in range(depth):
        j = i // 2
        last = i == depth - 1
        g_i = norm_g[i].reshape(1, d)
        if i % 2 == 0:
            w_in = dn_w_in[j].astype(BF16)
            w_main = w_in[:, :DN_CONV_DIM + DN_VAL_DIM]
            w_ab = jnp.pad(w_in[:, DN_CONV_DIM + DN_VAL_DIM:], ((0, 0), (0, LANES - 4 * DN_V_HEADS)))
            qkv_pre, z, ab = _dn_inproj(h, mod4, g_i, w_main, w_ab, i)
            alog_vec = _lane_pad(dn_a_log[j], 2 * DN_V_HEADS)
            dtb_vec = _lane_pad(dn_dt_bias[j], 2 * DN_V_HEADS)
            qkv, gates = _dn_conv(qkv_pre, ab, dn_conv_w[j], alog_vec, dtb_vec)
            gt = gates[:, :, :4 * DN_V_HEADS].reshape(b, lt, 2, 2, DN_QK_HEADS, rep)
            gate_rows = jnp.transpose(gt[:, :, ::-1], (0, 3, 4, 2, 5, 1)).reshape(
                b, 2, DN_QK_HEADS, 2 * rep, lt)
            o_f, o_b = _delta_rule(qkv, gate_rows)
            h = _outproj([o_f, o_b], z, h, mod4, dn_w_out[j].astype(BF16), i,
                         head_g=dn_o_norm_g[j].reshape(1, HEAD),
                         final_g=final_norm_g.reshape(1, d) if last else None)
        else:
            w_in = att_w_in[j].astype(BF16)
            w_vt = w_in[:, ATT_Q_DIM + ATT_KV_DIM:ATT_Q_DIM + 2 * ATT_KV_DIM].T
            q, k, vt, z = _att_inproj(h, mod4, g_i, w_in, w_vt,
                                      att_q_norm_g[j].reshape(1, HEAD), att_k_norm_g[j].reshape(1, HEAD),
                                      cos_t, sin_t, i)
            o = _attention(q, k, vt)
            h = _outproj([o], z, h, mod4, att_w_out[j].astype(BF16), i,
                         final_g=final_norm_g.reshape(1, d) if last else None)
    return h
```

```python
import functools
import math

import jax
import jax.numpy as jnp
from jax import lax
from jax.experimental import pallas as pl
from jax.experimental.pallas import tpu as pltpu

F32 = jnp.float32
BF16 = jnp.bfloat16

NORM_EPS = 1e-6
GRID_W = 64
ROPE_THETA = 10000.0

HEAD = 128
DN_QK_HEADS = 8
DN_V_HEADS = 16
DN_KEY_DIM = DN_QK_HEADS * HEAD
DN_VAL_DIM = DN_V_HEADS * HEAD
DN_CONV_DIM = 2 * DN_KEY_DIM + DN_VAL_DIM
DN_CONV_K = 5
DN_CHUNK = 64
ATT_Q_HEADS = 8
ATT_KV_HEADS = 2
ATT_GROUP = ATT_Q_HEADS // ATT_KV_HEADS
ATT_Q_DIM = ATT_Q_HEADS * HEAD
ATT_KV_DIM = ATT_KV_HEADS * HEAD

ROW_TILE = 256
LANES = 128
HALO = 8
MOD_ROWS = 16
N_COL_CHUNK = 512
DN_HEAD_BLOCK = 4
VMEM_LIMIT = 56 * 1024 * 1024
NEG_BIG = -1e30


def _cparams(*sem):
    return pltpu.CompilerParams(dimension_semantics=sem, vmem_limit_bytes=VMEM_LIMIT)


def _resident(shape):
    return pl.BlockSpec(shape, lambda *_: (0,) * len(shape), pipeline_mode=pl.Buffered(1))


def _silu(x):
    hx = 0.5 * x
    return hx + hx * jnp.tanh(hx)


def _mod_body(c_ref, w_ref, b_ref, o_ref):
    sc = _silu(c_ref[...])
    o_ref[...] = jnp.dot(sc, w_ref[...], preferred_element_type=F32) + b_ref[...]


def _modulation(cvec, ada_w, ada_b):
    depth, d, d3 = ada_w.shape
    tn = 1024
    return pl.pallas_call(
        _mod_body,
        grid=(depth, d3 // tn),
        in_specs=[
            pl.BlockSpec((MOD_ROWS, d), lambda i, j: (0, 0)),
            pl.BlockSpec((None, d, tn), lambda i, j: (i, 0, j)),
            pl.BlockSpec((None, 1, tn), lambda i, j: (i, 0, j)),
        ],
        out_specs=pl.BlockSpec((None, MOD_ROWS, tn), lambda i, j: (i, 0, j)),
        out_shape=jax.ShapeDtypeStruct((depth, MOD_ROWS, d3), F32),
        compiler_params=_cparams("arbitrary", "arbitrary"),
        name="adaln_modulation",
    )(cvec, ada_w, ada_b.reshape(depth, 1, d3))


def _mod_spec(layer, batch, d3):
    return pl.BlockSpec((None, None, 1, d3),
                        lambda b, r: (layer, jnp.where(r == 0, batch, b), 0, 0))


def _modulated_norm(x, g, mod, d):
    ms = jnp.mean(x * x, axis=-1, keepdims=True)
    xn = x * lax.rsqrt(ms + NORM_EPS) * g
    return xn * (1.0 + mod[:, d:2 * d]) + mod[:, 0:d]


def _dn_inproj_body(h_ref, mod_ref, g_ref, w_ref, wab_ref, qkv_ref, z_ref, ab_ref):
    d = h_ref.shape[-1]
    u = _modulated_norm(h_ref[...], g_ref[...], mod_ref[...], d).astype(BF16)
    for n in range(DN_CONV_DIM // N_COL_CHUNK):
        cs = slice(n * N_COL_CHUNK, (n + 1) * N_COL_CHUNK)
        qkv_ref[:, cs] = jnp.dot(u, w_ref[:, cs], preferred_element_type=F32)
    for n in range(DN_VAL_DIM // N_COL_CHUNK):
        cs = slice(n * N_COL_CHUNK, (n + 1) * N_COL_CHUNK)
        ws = slice(DN_CONV_DIM + n * N_COL_CHUNK, DN_CONV_DIM + (n + 1) * N_COL_CHUNK)
        z_ref[:, cs] = jnp.dot(u, w_ref[:, ws], preferred_element_type=F32).astype(BF16)
    ab_ref[...] = jnp.dot(u, wab_ref[...], preferred_element_type=F32)


def _dn_inproj(h, mod4, norm_g, w_main, w_ab, layer):
    b, lt, d = h.shape
    nt = lt // ROW_TILE
    d3 = mod4.shape[-1]
    row = lambda c: pl.BlockSpec((None, ROW_TILE, c), lambda bi, r: (bi, r, 0))
    return pl.pallas_call(
        _dn_inproj_body,
        grid=(b, nt),
        in_specs=[row(d), _mod_spec(layer, b, d3), _resident((1, d)),
                  _resident(w_main.shape), _resident(w_ab.shape)],
        out_specs=[row(DN_CONV_DIM), row(DN_VAL_DIM), row(LANES)],
        out_shape=[jax.ShapeDtypeStruct((b, lt, DN_CONV_DIM), F32),
                   jax.ShapeDtypeStruct((b, lt, DN_VAL_DIM), BF16),
                   jax.ShapeDtypeStruct((b, lt, LANES), F32)],
        compiler_params=_cparams("arbitrary", "arbitrary"),
        name="dn_inproj",
    )(h, mod4, norm_g, w_main, w_ab)


def _dn_conv_body(x_ref, prev_ref, next_ref, cw_ref, ab_ref, alog_ref, dtb_ref,
                  qkv_ref, gate_ref, pad_ref):
    r = pl.program_id(1)
    nt = pl.num_programs(1)
    has_prev = r > 1
    has_next = jnp.logical_and(r > 0, r < nt - 1)
    half = DN_CONV_K // 2
    for hb in range(DN_CONV_DIM // HEAD):
        cs = slice(hb * HEAD, (hb + 1) * HEAD)
        pad_ref[0:HALO, :] = jnp.where(has_prev, prev_ref[:, cs], 0.0)
        pad_ref[HALO:HALO + ROW_TILE, :] = x_ref[:, cs]
        pad_ref[HALO + ROW_TILE:, :] = jnp.where(has_next, next_ref[:, cs], 0.0)
        acc = None
        for j in range(DN_CONV_K):
            win = pad_ref[pl.ds(HALO - half + j, ROW_TILE), :]
            term = win * cw_ref[j:j + 1, cs]
            acc = term if acc is None else acc + term
        y = _silu(acc)
        if hb < 2 * DN_QK_HEADS:
            y = y * lax.rsqrt(jnp.sum(y * y, axis=-1, keepdims=True) + NORM_EPS)
        qkv_ref[:, cs] = y.astype(BF16)
    ab = ab_ref[...]
    lane = lax.broadcasted_iota(jnp.int32, ab.shape, 1)
    beta = jax.nn.sigmoid(ab)
    g = -jnp.exp(alog_ref[...]) * jax.nn.softplus(ab + dtb_ref[...])
    ti = lax.broadcasted_iota(jnp.int32, (ROW_TILE, ROW_TILE), 0)
    tj = lax.broadcasted_iota(jnp.int32, (ROW_TILE, ROW_TILE), 1)
    prefix_mat = jnp.where(jnp.logical_and(ti // DN_CHUNK == tj // DN_CHUNK, tj <= ti), 1.0, 0.0)
    prefix = jnp.dot(prefix_mat, g, precision=lax.Precision.HIGHEST, preferred_element_type=F32)
    row_chunk = lax.broadcasted_iota(jnp.int32, ab.shape, 0) // DN_CHUNK
    total = jnp.zeros_like(prefix)
    for ch in range(ROW_TILE // DN_CHUNK):
        end = (ch + 1) * DN_CHUNK
        total = jnp.where(row_chunk == ch, prefix[end - 1:end, :], total)
    backward_lane = lane >= 3 * DN_V_HEADS
    gcum = jnp.where(backward_lane, total - prefix + g, prefix)
    gate_ref[...] = jnp.where(lane < 2 * DN_V_HEADS, beta, gcum)


def _dn_conv(qkv_pre, ab, conv_w, alog_vec, dtb_vec):
    b, lt, c = qkv_pre.shape
    nt = lt // ROW_TILE
    per = ROW_TILE // HALO
    nhalo = lt // HALO
    row = lambda w: pl.BlockSpec((None, ROW_TILE, w), lambda bi, r: (bi, r, 0))
    return pl.pallas_call(
        _dn_conv_body,
        grid=(b, nt),
        in_specs=[
            row(c),
            pl.BlockSpec((None, HALO, c), lambda bi, r: (bi, jnp.maximum(r * per - 1, 0), 0)),
            pl.BlockSpec((None, HALO, c), lambda bi, r: (bi, jnp.minimum((r + 1) * per, nhalo - 1), 0)),
            _resident(conv_w.shape), row(LANES), _resident((1, LANES)), _resident((1, LANES)),
        ],
        out_specs=[row(c), row(LANES)],
        out_shape=[jax.ShapeDtypeStruct((b, lt, c), BF16), jax.ShapeDtypeStruct((b, lt, LANES), F32)],
        scratch_shapes=[pltpu.VMEM((ROW_TILE + 2 * HALO, HEAD), F32)],
        compiler_params=_cparams("arbitrary", "arbitrary"),
        name="dn_conv_gates",
    )(qkv_pre, qkv_pre, qkv_pre, conv_w, ab, alog_vec, dtb_vec)


def _dot(a, b):
    return jnp.dot(a.astype(BF16), b.astype(BF16), preferred_element_type=F32)


def _dot_nt(a, b):
    return lax.dot_general(a.astype(BF16), b.astype(BF16), (((1,), (1,)), ((), ())),
                           preferred_element_type=F32)


def _dot_tn(a, b):
    return lax.dot_general(a.astype(BF16), b.astype(BF16), (((0,), (0,)), ((), ())),
                           preferred_element_type=F32)


def _pair_diag(x):
    lane = lax.broadcasted_iota(jnp.int32, x.shape, 1)
    zero = jnp.zeros_like(x)
    c = x.shape[0]
    return jnp.concatenate([jnp.where(lane < c, x, zero), jnp.where(lane >= c, x, zero)], axis=0)


def _pair_matmul(a, x):
    return jnp.dot(a.astype(BF16), _pair_diag(x.astype(BF16)), preferred_element_type=F32)


def _delta_body(qf_ref, kf_ref, vf_ref, gf_ref, qb_ref, kb_ref, vb_ref, gb_ref,
                of_ref, ob_ref, s_ref):
    c = DN_CHUNK
    n_chunks = ROW_TILE // c
    rep = DN_V_HEADS // DN_QK_HEADS
    assert rep == 2 and 2 * c == LANES
    q_scale = HEAD ** -0.5

    @pl.when(pl.program_id(2) == 0)
    def _():
        s_ref[...] = jnp.zeros_like(s_ref)

    ci = lax.broadcasted_iota(jnp.int32, (c, LANES), 0)
    cl = lax.broadcasted_iota(jnp.int32, (c, LANES), 1)
    cj = cl % c
    head1 = cl >= c
    eye_p = jnp.where(ci == cj, 1.0, 0.0)

    walks = []
    for reverse, (q_ref, k_ref, v_ref, g_ref, o_ref) in enumerate(
            ((qf_ref, kf_ref, vf_ref, gf_ref, of_ref), (qb_ref, kb_ref, vb_ref, gb_ref, ob_ref))):
        for hb in range(DN_HEAD_BLOCK):
            gr = g_ref[hb]
            gr8 = jnp.concatenate([gr, jnp.zeros((8 - gr.shape[0], ROW_TILE), F32)], axis=0)
            walks.append(dict(
                reverse=reverse, hb=hb, q_ref=q_ref, k_ref=k_ref, v_ref=v_ref, o_ref=o_ref,
                qk_cols=slice(hb * HEAD, (hb + 1) * HEAD), v_col0=hb * rep * HEAD,
                row=gr8, col=gr8.T,
                incl=(cj >= ci) if reverse else (cj <= ci),
                strict=(cj > ci) if reverse else (cj < ci),
                order=list(range(n_chunks - 1, -1, -1) if reverse else range(n_chunks))))

    def prepare(pairs, items):
        for w, ch in pairs:
            rows = slice(ch * c, (ch + 1) * c)
            gcol = [w["col"][rows, s:s + 1] for s in range(rep)]
            bcol = [w["col"][rows, rep + s:rep + s + 1] for s in range(rep)]
            last = 0 if w["reverse"] else c - 1
            items.append(dict(w=w, rows=rows, gcol=gcol, bcol=bcol,
                              gtot=[g[last:last + 1, :] for g in gcol]))
        yield
        for it in items:
            w, rows = it["w"], it["rows"]
            kc = w["k_ref"][rows, w["qk_cols"]]
            qc = w["q_ref"][rows, w["qk_cols"]]
            k2 = jnp.concatenate([kc, kc], axis=0)
            it.update(kf=kc.astype(F32), qf=qc.astype(F32), kk=_dot_nt(kc, k2), qk=_dot_nt(qc, k2))
        yield
        for it in items:
            w = it["w"]
            grow_p = jnp.concatenate([w["row"][s:s + 1, it["rows"]] for s in range(rep)], axis=1)
            gcol_p = jnp.where(head1, it["gcol"][1], it["gcol"][0])
            it["beta_p"] = jnp.where(head1, it["bcol"][1], it["bcol"][0])
            it["decay_p"] = jnp.exp(jnp.where(w["incl"], gcol_p - grow_p, NEG_BIG))
        yield
        for it in items:
            it["a"] = jnp.where(it["w"]["strict"], it["beta_p"] * it.pop("kk") * it["decay_p"], 0.0)
            it["qkm"] = q_scale * it.pop("qk") * it.pop("decay_p")
        yield

    def invert(items):
        for it in items:
            it["x"] = _pair_matmul(it["a"], it["a"])
            it["t"] = eye_p - it["a"]
        yield
        for _ in range(int(math.log2(c)) - 3):
            for it in items:
                r2 = _pair_matmul(jnp.concatenate([it["x"], it["t"]], axis=0), it["x"])
                it["x"] = r2[:c]
                it["t"] = it["t"] + r2[c:]
            yield
        for it in items:
            it["t"] = it["t"] + _pair_matmul(it["t"], it["x"])
        yield
        for it in items:
            it["res"] = eye_p - it["t"] - _pair_matmul(it["a"], it["t"])
        yield
        for it in items:
            it["t"] = it["t"] + _pair_matmul(it["t"], it.pop("res"))
        yield

    def chunk_products(items):
        for it in items:
            rhs = []
            for s in range(rep):
                eg = jnp.exp(it["gcol"][s])
                v0 = it["w"]["v_col0"] + s * HEAD
                vf = it["w"]["v_ref"][it["rows"], v0:v0 + HEAD].astype(F32)
                rhs.append(jnp.concatenate([vf * it["bcol"][s], it["kf"] * (it["bcol"][s] * eg)], axis=1))
            it["rhs"] = jnp.concatenate(rhs, axis=0).astype(BF16)
            it["k_dec"] = [(it["kf"] * jnp.exp(it["gtot"][s] - it["gcol"][s])).astype(BF16) for s in range(rep)]
            it["qd"] = [it["qf"] * (q_scale * jnp.exp(it["gcol"][s])) for s in range(rep)]
            it["e_tot"] = [jnp.exp(it["gtot"][s]) for s in range(rep)]
        yield
        for it in items:
            it["uw"] = _dot(_pair_diag(it["t"]), it.pop("rhs"))
        yield
        for it in items:
            it["qo"] = _dot(_pair_diag(it["qkm"]), it["uw"])
        yield
        for it in items:
            it["gn"] = [_dot_tn(it["k_dec"][s], it["uw"][s * c:(s + 1) * c])
                        for s in range(rep)]
        yield
        for it in items:
            it["q_eff"] = [it["qd"][s] - it["qo"][s * c:(s + 1) * c, HEAD:] for s in range(rep)]
        yield

    def advance(items, n_walks):
        for first in range(0, len(items), n_walks):
            for it in items[first:first + n_walks]:
                state = it["w"]["state"]
                for s in range(rep):
                    rs = _dot(jnp.concatenate([it["gn"][s][:, HEAD:], it["q_eff"][s]], axis=0), state[s])
                    state[s] = state[s] * it["e_tot"][s] - rs[:HEAD] + it["gn"][s][:, :HEAD]
                    v0 = it["w"]["v_col0"] + s * HEAD
                    it["w"]["o_ref"][it["rows"], v0:v0 + HEAD] = (
                        rs[HEAD:] + it["qo"][s * c:(s + 1) * c, :HEAD]).astype(BF16)
            yield

    def emit(*stages):
        live = list(stages)
        while live:
            for g in list(live):
                if next(g, live) is live:
                    live.remove(g)

    for w in walks:
        w["state"] = [s_ref[w["reverse"], w["hb"], s] for s in range(rep)]
    items = []
    emit(prepare([(w, w["order"][rank]) for rank in range(n_chunks) for w in walks], items))
    emit(invert(items))
    emit(chunk_products(items))
    emit(advance(items, len(walks)))
    for w in walks:
        for s in range(rep):
            s_ref[w["reverse"], w["hb"], s] = w["state"][s]


def _delta_rule(qkv, gate_rows):
    b, lt, _ = qkv.shape
    nt = lt // ROW_TILE
    rep = DN_V_HEADS // DN_QK_HEADS
    v_col0 = 2 * DN_KEY_DIM // (DN_HEAD_BLOCK * rep * HEAD)
    fwd = lambda r: r
    bwd = lambda r: jnp.where(r == 0, 0, nt - r)

    hb = DN_HEAD_BLOCK
    n_blocks = DN_QK_HEADS // hb

    def specs(tile, d):
        return [
            pl.BlockSpec((None, ROW_TILE, hb * HEAD), lambda bi, j, r: (bi, tile(r), j)),
            pl.BlockSpec((None, ROW_TILE, hb * HEAD), lambda bi, j, r: (bi, tile(r), n_blocks + j)),
            pl.BlockSpec((None, ROW_TILE, hb * rep * HEAD), lambda bi, j, r: (bi, tile(r), v_col0 + j)),
            pl.BlockSpec((None, None, hb, 2 * rep, ROW_TILE), lambda bi, j, r: (bi, d, j, 0, tile(r))),
        ]

    out = lambda tile: pl.BlockSpec((None, ROW_TILE, hb * rep * HEAD), lambda bi, j, r: (bi, tile(r), j))
    o_shape = jax.ShapeDtypeStruct((b, lt, DN_VAL_DIM), BF16)
    return pl.pallas_call(
        _delta_body,
        grid=(b, n_blocks, nt),
        in_specs=specs(fwd, 0) + specs(bwd, 1),
        out_specs=[out(fwd), out(bwd)],
        out_shape=[o_shape, o_shape],
        scratch_shapes=[pltpu.VMEM((2, hb, rep, HEAD, HEAD), F32)],
        compiler_params=_cparams("arbitrary", "arbitrary", "arbitrary"),
        name="dn_delta_rule",
    )(qkv, qkv, qkv, gate_rows, qkv, qkv, qkv, gate_rows)


def _outproj_body(*refs, n_o, head_norm, final_norm):
    o_refs = refs[:n_o]
    z_ref, h_ref, mod_ref, w_ref = refs[n_o:n_o + 4]
    rest = list(refs[n_o + 4:])
    og_ref = rest.pop(0) if head_norm else None
    fg_ref = rest.pop(0) if final_norm else None
    out_ref = rest.pop(0)
    d = h_ref.shape[-1]
    width = z_ref.shape[-1]
    parts = []
    for hb in range(width // HEAD):
        cs = slice(hb * HEAD, (hb + 1) * HEAD)
        o = o_refs[0][:, cs].astype(F32)
        for extra in o_refs[1:]:
            o = o + extra[:, cs].astype(F32)
        if head_norm:
            o = o * lax.rsqrt(jnp.mean(o * o, axis=-1, keepdims=True) + NORM_EPS) * og_ref[...]
        parts.append((o * _silu(z_ref[:, cs].astype(F32))).astype(BF16))
    y = jnp.dot(jnp.concatenate(parts, axis=1), w_ref[...], preferred_element_type=F32)
    hn = h_ref[...] + mod_ref[:, 2 * d:3 * d] * y
    if final_norm:
        hn = hn * lax.rsqrt(jnp.mean(hn * hn, axis=-1, keepdims=True) + NORM_EPS) * fg_ref[...]
    out_ref[...] = hn


def _outproj(o_list, z, h, mod4, w_out, layer, head_g=None, final_g=None):
    b, lt, d = h.shape
    nt = lt // ROW_TILE
    d3 = mod4.shape[-1]
    width = z.shape[-1]
    skip = 1 if final_g is not None else 0
    row = lambda c: pl.BlockSpec((None, ROW_TILE, c), lambda bi, r: (bi, r + skip, 0))
    mod_spec = pl.BlockSpec((None, None, 1, d3),
                            lambda bi, r: (layer, jnp.where(r + skip == 0, b, bi), 0, 0))
    in_specs = [row(width)] * len(o_list) + [row(width), row(d), mod_spec, _resident(w_out.shape)]
    args = list(o_list) + [z, h, mod4, w_out]
    if head_g is not None:
        in_specs.append(_resident((1, HEAD)))
        args.append(head_g)
    if final_g is not None:
        in_specs.append(_resident((1, d)))
        args.append(final_g)
    return pl.pallas_call(
        functools.partial(_outproj_body, n_o=len(o_list), head_norm=head_g is not None,
                          final_norm=final_g is not None),
        grid=(b, nt - skip),
        in_specs=in_specs,
        out_specs=pl.BlockSpec((None, ROW_TILE, d), lambda bi, r: (bi, r, 0)),
        out_shape=jax.ShapeDtypeStruct((b, lt - skip * ROW_TILE, d), F32),
        compiler_params=_cparams("arbitrary", "arbitrary"),
        name="mixer_outproj",
    )(*args)


def _rope_partner(x):
    lane = lax.broadcasted_iota(jnp.int32, x.shape, 1)
    quarter = HEAD // 4
    return jnp.where((lane % (2 * quarter)) < quarter,
                     pltpu.roll(x, HEAD - quarter, axis=1), pltpu.roll(x, quarter, axis=1))


def _att_inproj_body(h_ref, mod_ref, g_ref, w_ref, wvt_ref, qg_ref, kg_ref, cos_ref, sin_ref,
                     q_ref, k_ref, vt_ref, z_ref):
    d = h_ref.shape[-1]
    u = _modulated_norm(h_ref[...], g_ref[...], mod_ref[...], d).astype(BF16)
    cos = cos_ref[...]
    sin = sin_ref[...]
    q_scale = (HEAD ** -0.5) * math.log2(math.e)

    def normed_rope(x, g):
        xn = x * lax.rsqrt(jnp.mean(x * x, axis=-1, keepdims=True) + NORM_EPS) * g
        return xn * cos + _rope_partner(xn) * sin

    pair = 2 * HEAD
    for pb in range(ATT_Q_DIM // pair):
        x2 = jnp.dot(u, w_ref[:, pb * pair:(pb + 1) * pair], preferred_element_type=F32)
        for t in range(2):
            cs = slice(pb * pair + t * HEAD, pb * pair + (t + 1) * HEAD)
            q_ref[:, cs] = (normed_rope(x2[:, t * HEAD:(t + 1) * HEAD], qg_ref[...]) * q_scale).astype(BF16)
    for pb in range(ATT_KV_DIM // pair):
        x2 = jnp.dot(u, w_ref[:, ATT_Q_DIM + pb * pair:ATT_Q_DIM + (pb + 1) * pair],
                     preferred_element_type=F32)
        for t in range(2):
            cs = slice(pb * pair + t * HEAD, pb * pair + (t + 1) * HEAD)
            k_ref[:, cs] = normed_rope(x2[:, t * HEAD:(t + 1) * HEAD], kg_ref[...]).astype(BF16)
    vt = lax.dot_general(wvt_ref[...], u, (((1,), (1,)), ((), ())), preferred_element_type=F32)
    for hk in range(ATT_KV_HEADS):
        vt_ref[hk] = vt[hk * HEAD:(hk + 1) * HEAD, :].astype(BF16)
    z0 = ATT_Q_DIM + 2 * ATT_KV_DIM
    for n in range(ATT_Q_DIM // N_COL_CHUNK):
        cs = slice(n * N_COL_CHUNK, (n + 1) * N_COL_CHUNK)
        ws = slice(z0 + n * N_COL_CHUNK, z0 + (n + 1) * N_COL_CHUNK)
        z_ref[:, cs] = jnp.dot(u, w_ref[:, ws], preferred_element_type=F32).astype(BF16)


def _att_inproj(h, mod4, norm_g, w_in, w_vt, q_g, k_g, cos_t, sin_t, layer):
    b, lt, d = h.shape
    nt = lt // ROW_TILE
    d3 = mod4.shape[-1]
    row = lambda c: pl.BlockSpec((None, ROW_TILE, c), lambda bi, r: (bi, r, 0))
    tab = pl.BlockSpec((ROW_TILE, HEAD), lambda bi, r: (r, 0))
    return pl.pallas_call(
        _att_inproj_body,
        grid=(b, nt),
        in_specs=[row(d), _mod_spec(layer, b, d3), _resident((1, d)), _resident(w_in.shape),
                  _resident(w_vt.shape), _resident((1, HEAD)), _resident((1, HEAD)), tab, tab],
        out_specs=[row(ATT_Q_DIM), row(ATT_KV_DIM),
                   pl.BlockSpec((None, ATT_KV_HEADS, None, HEAD, ROW_TILE), lambda bi, r: (bi, 0, r, 0, 0)),
                   row(ATT_Q_DIM)],
        out_shape=[jax.ShapeDtypeStruct((b, lt, ATT_Q_DIM), BF16),
                   jax.ShapeDtypeStruct((b, lt, ATT_KV_DIM), BF16),
                   jax.ShapeDtypeStruct((b, ATT_KV_HEADS, nt, HEAD, ROW_TILE), BF16),
                   jax.ShapeDtypeStruct((b, lt, ATT_Q_DIM), BF16)],
        compiler_params=_cparams("arbitrary", "arbitrary"),
        name="att_inproj",
    )(h, mod4, norm_g, w_in, w_vt, q_g, k_g, cos_t, sin_t)


def _attn_body(q_ref, k_ref, vt_ref, o_ref, s_ref):
    sub = 8
    fold = lambda x, op: op(x.reshape(ROW_TILE // sub, sub, ROW_TILE), axis=0)

    def score_chunk(g, m, chunk):
        hk = g // ATT_GROUP
        st = lax.dot_general(k_ref[chunk * ROW_TILE:(chunk + 1) * ROW_TILE, hk * HEAD:(hk + 1) * HEAD],
                             q_ref[:, g * HEAD:(g + 1) * HEAD],
                             (((1,), (1,)), ((), ())), preferred_element_type=F32)
        s_ref[g % 2, chunk] = st
        return jnp.maximum(m, fold(st, jnp.max))

    def value_chunk(g, m_row, l, acc, chunk):
        pt = jnp.exp2(s_ref[g % 2, chunk] - m_row)
        l = l + fold(pt, jnp.sum)
        return l, acc + jnp.dot(vt_ref[g // ATT_GROUP, chunk], pt.astype(BF16),
                                preferred_element_type=F32)

    def phase(n_chunks, g_score, g_value, m_row):
        m = jnp.full((sub, ROW_TILE), NEG_BIG, F32)
        l = jnp.zeros((sub, ROW_TILE), F32)
        acc = jnp.zeros((HEAD, ROW_TILE), F32)
        for chunk in range(n_chunks):
            if g_score is not None:
                m = score_chunk(g_score, m, chunk)
            if g_value is not None:
                l, acc = value_chunk(g_value, m_row, l, acc, chunk)
        if g_value is not None:
            o_t = acc / jnp.sum(l, axis=0, keepdims=True)
            o_ref[:, g_value * HEAD:(g_value + 1) * HEAD] = o_t.T.astype(BF16)
        return jnp.max(m, axis=0, keepdims=True) if g_score is not None else None

    def attend(n_chunks):
        m_row = phase(n_chunks, 0, None, None)
        for g in range(ATT_Q_HEADS):
            m_row = phase(n_chunks, g + 1 if g + 1 < ATT_Q_HEADS else None, g, m_row)

    is_ctx = pl.program_id(1) == 0
    pl.when(is_ctx)(functools.partial(attend, 1))
    pl.when(jnp.logical_not(is_ctx))(functools.partial(attend, k_ref.shape[0] // ROW_TILE))


def _attention(q, k, vt):
    b, lt, _ = q.shape
    nt = lt // ROW_TILE
    return pl.pallas_call(
        _attn_body,
        grid=(b, nt),
        in_specs=[
            pl.BlockSpec((None, ROW_TILE, ATT_Q_DIM), lambda bi, r: (bi, r, 0)),
            pl.BlockSpec((None, lt, ATT_KV_DIM), lambda bi, r: (bi, 0, 0)),
            pl.BlockSpec((None, ATT_KV_HEADS, nt, HEAD, ROW_TILE), lambda bi, r: (bi, 0, 0, 0, 0)),
        ],
        out_specs=pl.BlockSpec((None, ROW_TILE, ATT_Q_DIM), lambda bi, r: (bi, r, 0)),
        out_shape=jax.ShapeDtypeStruct((b, lt, ATT_Q_DIM), BF16),
        scratch_shapes=[pltpu.VMEM((2, nt, ROW_TILE, ROW_TILE), F32)],
        compiler_params=_cparams("arbitrary", "arbitrary"),
        name="gqa_attention",
    )(q, k, vt)


def _rope_tables(n_lat, n_ctx):
    t = jnp.arange(n_lat)
    row = (t // GRID_W).astype(F32)
    col = (t % GRID_W).astype(F32)
    axis_dim = HEAD // 2
    inv = ROPE_THETA ** (-jnp.arange(0, axis_dim, 2, dtype=F32) / axis_dim)
    ang_r = row[:, None] * inv
    ang_c = col[:, None] * inv
    cos = jnp.concatenate([jnp.cos(ang_r)] * 2 + [jnp.cos(ang_c)] * 2, axis=-1)
    sin = jnp.concatenate([-jnp.sin(ang_r), jnp.sin(ang_r), -jnp.sin(ang_c), jnp.sin(ang_c)], axis=-1)
    cos = jnp.concatenate([jnp.ones((n_ctx, HEAD), F32), cos], axis=0)
    sin = jnp.concatenate([jnp.zeros((n_ctx, HEAD), F32), sin], axis=0)
    return cos, sin


def _lane_pad(vec, offset):
    out = jnp.zeros((1, LANES), F32)
    return lax.dynamic_update_slice(out, vec.reshape(1, -1).astype(F32), (0, offset))


def kernel(x, c, ctx, c_ctx, norm_g, ada_w, ada_b, dn_w_in, dn_conv_w, dn_a_log, dn_dt_bias,
           dn_o_norm_g, dn_w_out, att_w_in, att_q_norm_g, att_k_norm_g, att_w_out, final_norm_g):
    b, n_lat, d = x.shape
    n_ctx = ctx.shape[1]
    depth = norm_g.shape[0]
    assert n_ctx == ROW_TILE and n_lat % ROW_TILE == 0 and b < MOD_ROWS and d % LANES == 0
    lt = n_ctx + n_lat
    rep = DN_V_HEADS // DN_QK_HEADS

    cvec = jnp.concatenate([c, c_ctx[None, :], jnp.zeros((MOD_ROWS - b - 1, d), F32)], axis=0)
    mod4 = _modulation(cvec, ada_w, ada_b).reshape(depth, MOD_ROWS, 1, 3 * d)
    h = jnp.concatenate([ctx, x], axis=1)
    cos_t, sin_t = _rope_tables(n_lat, n_ctx)

    for i in range(depth):
        j = i // 2
        last = i == depth - 1
        g_i = norm_g[i].reshape(1, d)
        if i % 2 == 0:
            w_in = dn_w_in[j].astype(BF16)
            w_main = w_in[:, :DN_CONV_DIM + DN_VAL_DIM]
            w_ab = jnp.pad(w_in[:, DN_CONV_DIM + DN_VAL_DIM:], ((0, 0), (0, LANES - 4 * DN_V_HEADS)))
            qkv_pre, z, ab = _dn_inproj(h, mod4, g_i, w_main, w_ab, i)
            alog_vec = _lane_pad(dn_a_log[j], 2 * DN_V_HEADS)
            dtb_vec = _lane_pad(dn_dt_bias[j], 2 * DN_V_HEADS)
            qkv, gates = _dn_conv(qkv_pre, ab, dn_conv_w[j], alog_vec, dtb_vec)
            gt = gates[:, :, :4 * DN_V_HEADS].reshape(b, lt, 2, 2, DN_QK_HEADS, rep)
            gate_rows = jnp.transpose(gt[:, :, ::-1], (0, 3, 4, 2, 5, 1)).reshape(
                b, 2, DN_QK_HEADS, 2 * rep, lt)
            o_f, o_b = _delta_rule(qkv, gate_rows)
            h = _outproj([o_f, o_b], z, h, mod4, dn_w_out[j].astype(BF16), i,
                         head_g=dn_o_norm_g[j].reshape(1, HEAD),
                         final_g=final_norm_g.reshape(1, d) if last else None)
        else:
            w_in = att_w_in[j].astype(BF16)
            w_vt = w_in[:, ATT_Q_DIM + ATT_KV_DIM:ATT_Q_DIM + 2 * ATT_KV_DIM].T
            q, k, vt, z = _att_inproj(h, mod4, g_i, w_in, w_vt,
                                      att_q_norm_g[j].reshape(1, HEAD), att_k_norm_g[j].reshape(1, HEAD),
                                      cos_t, sin_t, i)
            o = _attention(q, k, vt)
            h = _outproj([o], z, h, mod4, att_w_out[j].astype(BF16), i,
                         final_g=final_norm_g.reshape(1, d) if last else None)
    return h
```

```python
import functools
import math

import jax
import jax.numpy as jnp
from jax import lax
from jax.experimental import pallas as pl
from jax.experimental.pallas import tpu as pltpu

F32 = jnp.float32
BF16 = jnp.bfloat16

NORM_EPS = 1e-6
GRID_W = 64
ROPE_THETA = 10000.0

HEAD = 128
DN_QK_HEADS = 8
DN_V_HEADS = 16
DN_KEY_DIM = DN_QK_HEADS * HEAD
DN_VAL_DIM = DN_V_HEADS * HEAD
DN_CONV_DIM = 2 * DN_KEY_DIM + DN_VAL_DIM
DN_CONV_K = 5
DN_CHUNK = 64
ATT_Q_HEADS = 8
ATT_KV_HEADS = 2
ATT_GROUP = ATT_Q_HEADS // ATT_KV_HEADS
ATT_Q_DIM = ATT_Q_HEADS * HEAD
ATT_KV_DIM = ATT_KV_HEADS * HEAD

ROW_TILE = 256
LANES = 128
HALO = 8
MOD_ROWS = 16
N_COL_CHUNK = 512
DN_HEAD_BLOCK = 4
VMEM_LIMIT = 56 * 1024 * 1024
NEG_BIG = -1e30


def _cparams(*sem):
    return pltpu.CompilerParams(dimension_semantics=sem, vmem_limit_bytes=VMEM_LIMIT)


def _resident(shape):
    return pl.BlockSpec(shape, lambda *_: (0,) * len(shape), pipeline_mode=pl.Buffered(1))


def _silu(x):
    hx = 0.5 * x
    return hx + hx * jnp.tanh(hx)


def _mod_body(c_ref, w_ref, b_ref, o_ref):
    sc = _silu(c_ref[...])
    o_ref[...] = jnp.dot(sc, w_ref[...], preferred_element_type=F32) + b_ref[...]


def _modulation(cvec, ada_w, ada_b):
    depth, d, d3 = ada_w.shape
    tn = 1024
    return pl.pallas_call(
        _mod_body,
        grid=(depth, d3 // tn),
        in_specs=[
            pl.BlockSpec((MOD_ROWS, d), lambda i, j: (0, 0)),
            pl.BlockSpec((None, d, tn), lambda i, j: (i, 0, j)),
            pl.BlockSpec((None, 1, tn), lambda i, j: (i, 0, j)),
        ],
        out_specs=pl.BlockSpec((None, MOD_ROWS, tn), lambda i, j: (i, 0, j)),
        out_shape=jax.ShapeDtypeStruct((depth, MOD_ROWS, d3), F32),
        compiler_params=_cparams("arbitrary", "arbitrary"),
        name="adaln_modulation",
    )(cvec, ada_w, ada_b.reshape(depth, 1, d3))


def _mod_spec(layer, batch, d3):
    return pl.BlockSpec((None, None, 1, d3),
                        lambda b, r: (layer, jnp.where(r == 0, batch, b), 0, 0))


def _modulated_norm(x, g, mod, d):
    ms = jnp.mean(x * x, axis=-1, keepdims=True)
    xn = x * lax.rsqrt(ms + NORM_EPS) * g
    return xn * (1.0 + mod[:, d:2 * d]) + mod[:, 0:d]


def _stream_specs(hs, skip=0):
    d = hs[0].shape[-1]
    if len(hs) == 1:
        return [pl.BlockSpec((None, ROW_TILE, d), lambda bi, r: (bi, r + skip, 0))]
    return [pl.BlockSpec((None, ROW_TILE, d), lambda bi, r: (bi, 0, 0)),
            pl.BlockSpec((None, ROW_TILE, d), lambda bi, r: (bi, jnp.maximum(r + skip - 1, 0), 0))]


def _stream_rows(h_refs, skip=0):
    if len(h_refs) == 1:
        return h_refs[0][...]
    return jnp.where(pl.program_id(1) + skip == 0, h_refs[0][...], h_refs[1][...])


def _dn_inproj_body(*refs, n_h):
    h_refs = refs[:n_h]
    mod_ref, g_ref, w_ref, wab_ref, qkv_ref, z_ref, ab_ref = refs[n_h:]
    d = h_refs[0].shape[-1]
    u = _modulated_norm(_stream_rows(h_refs), g_ref[...], mod_ref[...], d).astype(BF16)
    for n in range(DN_CONV_DIM // N_COL_CHUNK):
        cs = slice(n * N_COL_CHUNK, (n + 1) * N_COL_CHUNK)
        qkv_ref[:, cs] = jnp.dot(u, w_ref[:, cs], preferred_element_type=F32)
    for n in range(DN_VAL_DIM // N_COL_CHUNK):
        cs = slice(n * N_COL_CHUNK, (n + 1) * N_COL_CHUNK)
        ws = slice(DN_CONV_DIM + n * N_COL_CHUNK, DN_CONV_DIM + (n + 1) * N_COL_CHUNK)
        z_ref[:, cs] = jnp.dot(u, w_ref[:, ws], preferred_element_type=F32).astype(BF16)
    ab_ref[...] = jnp.dot(u, wab_ref[...], preferred_element_type=F32)


def _dn_inproj(hs, mod4, norm_g, w_main, w_ab, layer):
    b, _, d = hs[0].shape
    lt = sum(a.shape[1] for a in hs)
    nt = lt // ROW_TILE
    d3 = mod4.shape[-1]
    row = lambda c: pl.BlockSpec((None, ROW_TILE, c), lambda bi, r: (bi, r, 0))
    return pl.pallas_call(
        functools.partial(_dn_inproj_body, n_h=len(hs)),
        grid=(b, nt),
        in_specs=_stream_specs(hs) + [_mod_spec(layer, b, d3), _resident((1, d)),
                                      _resident(w_main.shape), _resident(w_ab.shape)],
        out_specs=[row(DN_CONV_DIM), row(DN_VAL_DIM), row(LANES)],
        out_shape=[jax.ShapeDtypeStruct((b, lt, DN_CONV_DIM), F32),
                   jax.ShapeDtypeStruct((b, lt, DN_VAL_DIM), BF16),
                   jax.ShapeDtypeStruct((b, lt, LANES), F32)],
        compiler_params=_cparams("arbitrary", "arbitrary"),
        name="dn_inproj",
    )(*hs, mod4, norm_g, w_main, w_ab)


def _dn_conv_body(x_ref, prev_ref, next_ref, cw_ref, ab_ref, alog_ref, dtb_ref,
                  qkv_ref, gate_ref, pad_ref):
    r = pl.program_id(1)
    nt = pl.num_programs(1)
    has_prev = r > 1
    has_next = jnp.logical_and(r > 0, r < nt - 1)
    half = DN_CONV_K // 2
    for hb in range(DN_CONV_DIM // HEAD):
        cs = slice(hb * HEAD, (hb + 1) * HEAD)
        pad_ref[0:HALO, :] = jnp.where(has_prev, prev_ref[:, cs], 0.0)
        pad_ref[HALO:HALO + ROW_TILE, :] = x_ref[:, cs]
        pad_ref[HALO + ROW_TILE:, :] = jnp.where(has_next, next_ref[:, cs], 0.0)
        acc = None
        for j in range(DN_CONV_K):
            win = pad_ref[pl.ds(HALO - half + j, ROW_TILE), :]
            term = win * cw_ref[j:j + 1, cs]
            acc = term if acc is None else acc + term
        y = _silu(acc)
        if hb < 2 * DN_QK_HEADS:
            y = y * lax.rsqrt(jnp.sum(y * y, axis=-1, keepdims=True) + NORM_EPS)
        qkv_ref[:, cs] = y.astype(BF16)
    ab = ab_ref[...]
    lane = lax.broadcasted_iota(jnp.int32, ab.shape, 1)
    beta = jax.nn.sigmoid(ab)
    g = -jnp.exp(alog_ref[...]) * jax.nn.softplus(ab + dtb_ref[...])
    ti = lax.broadcasted_iota(jnp.int32, (ROW_TILE, ROW_TILE), 0)
    tj = lax.broadcasted_iota(jnp.int32, (ROW_TILE, ROW_TILE), 1)
    prefix_mat = jnp.where(jnp.logical_and(ti // DN_CHUNK == tj // DN_CHUNK, tj <= ti), 1.0, 0.0)
    prefix = jnp.dot(prefix_mat, g, precision=lax.Precision.HIGHEST, preferred_element_type=F32)
    row_chunk = lax.broadcasted_iota(jnp.int32, ab.shape, 0) // DN_CHUNK
    total = jnp.zeros_like(prefix)
    for ch in range(ROW_TILE // DN_CHUNK):
        end = (ch + 1) * DN_CHUNK
        total = jnp.where(row_chunk == ch, prefix[end - 1:end, :], total)
    backward_lane = lane >= 3 * DN_V_HEADS
    gcum = jnp.where(backward_lane, total - prefix + g, prefix)
    gate_ref[...] = jnp.where(lane < 2 * DN_V_HEADS, beta, gcum)


def _dn_conv(qkv_pre, ab, conv_w, alog_vec, dtb_vec):
    b, lt, c = qkv_pre.shape
    nt = lt // ROW_TILE
    per = ROW_TILE // HALO
    nhalo = lt // HALO
    row = lambda w: pl.BlockSpec((None, ROW_TILE, w), lambda bi, r: (bi, r, 0))
    return pl.pallas_call(
        _dn_conv_body,
        grid=(b, nt),
        in_specs=[
            row(c),
            pl.BlockSpec((None, HALO, c), lambda bi, r: (bi, jnp.maximum(r * per - 1, 0), 0)),
            pl.BlockSpec((None, HALO, c), lambda bi, r: (bi, jnp.minimum((r + 1) * per, nhalo - 1), 0)),
            _resident(conv_w.shape), row(LANES), _resident((1, LANES)), _resident((1, LANES)),
        ],
        out_specs=[row(c), row(LANES)],
        out_shape=[jax.ShapeDtypeStruct((b, lt, c), BF16), jax.ShapeDtypeStruct((b, lt, LANES), F32)],
        scratch_shapes=[pltpu.VMEM((ROW_TILE + 2 * HALO, HEAD), F32)],
        compiler_params=_cparams("arbitrary", "arbitrary"),
        name="dn_conv_gates",
    )(qkv_pre, qkv_pre, qkv_pre, conv_w, ab, alog_vec, dtb_vec)


def _dot(a, b):
    return jnp.dot(a.astype(BF16), b.astype(BF16), preferred_element_type=F32)


def _dot_nt(a, b):
    return lax.dot_general(a.astype(BF16), b.astype(BF16), (((1,), (1,)), ((), ())),
                           preferred_element_type=F32)


def _dot_tn(a, b):
    return lax.dot_general(a.astype(BF16), b.astype(BF16), (((0,), (0,)), ((), ())),
                           preferred_element_type=F32)


def _pair_diag(x):
    lane = lax.broadcasted_iota(jnp.int32, x.shape, 1)
    zero = jnp.zeros_like(x)
    c = x.shape[0]
    return jnp.concatenate([jnp.where(lane < c, x, zero), jnp.where(lane >= c, x, zero)], axis=0)


def _pair_matmul(a, x):
    return jnp.dot(a.astype(BF16), _pair_diag(x.astype(BF16)), preferred_element_type=F32)


def _delta_body(qf_ref, kf_ref, vf_ref, gf_ref, qb_ref, kb_ref, vb_ref, gb_ref,
                of_ref, ob_ref, s_ref):
    c = DN_CHUNK
    n_chunks = ROW_TILE // c
    rep = DN_V_HEADS // DN_QK_HEADS
    assert rep == 2 and 2 * c == LANES
    q_scale = HEAD ** -0.5

    @pl.when(pl.program_id(2) == 0)
    def _():
        s_ref[...] = jnp.zeros_like(s_ref)

    ci = lax.broadcasted_iota(jnp.int32, (c, LANES), 0)
    cl = lax.broadcasted_iota(jnp.int32, (c, LANES), 1)
    cj = cl % c
    head1 = cl >= c
    eye_p = jnp.where(ci == cj, 1.0, 0.0)

    walks = []
    for reverse, (q_ref, k_ref, v_ref, g_ref, o_ref) in enumerate(
            ((qf_ref, kf_ref, vf_ref, gf_ref, of_ref), (qb_ref, kb_ref, vb_ref, gb_ref, ob_ref))):
        for hb in range(DN_HEAD_BLOCK):
            gr = g_ref[hb]
            gr8 = jnp.concatenate([gr, jnp.zeros((8 - gr.shape[0], ROW_TILE), F32)], axis=0)
            walks.append(dict(
                reverse=reverse, hb=hb, q_ref=q_ref, k_ref=k_ref, v_ref=v_ref, o_ref=o_ref,
                qk_cols=slice(hb * HEAD, (hb + 1) * HEAD), v_col0=hb * rep * HEAD,
                row=gr8, col=gr8.T,
                incl=(cj >= ci) if reverse else (cj <= ci),
                strict=(cj > ci) if reverse else (cj < ci),
                order=list(range(n_chunks - 1, -1, -1) if reverse else range(n_chunks))))

    def prepare(pairs, items):
        for w, ch in pairs:
            rows = slice(ch * c, (ch + 1) * c)
            gcol = [w["col"][rows, s:s + 1] for s in range(rep)]
            bcol = [w["col"][rows, rep + s:rep + s + 1] for s in range(rep)]
            last = 0 if w["reverse"] else c - 1
            items.append(dict(w=w, rows=rows, gcol=gcol, bcol=bcol,
                              gtot=[g[last:last + 1, :] for g in gcol]))
        yield
        for it in items:
            w, rows = it["w"], it["rows"]
            kc = w["k_ref"][rows, w["qk_cols"]]
            qc = w["q_ref"][rows, w["qk_cols"]]
            k2 = jnp.concatenate([kc, kc], axis=0)
            it.update(kf=kc.astype(F32), qf=qc.astype(F32), kk=_dot_nt(kc, k2), qk=_dot_nt(qc, k2))
        yield
        for it in items:
            w = it["w"]
            grow_p = jnp.concatenate([w["row"][s:s + 1, it["rows"]] for s in range(rep)], axis=1)
            gcol_p = jnp.where(head1, it["gcol"][1], it["gcol"][0])
            it["beta_p"] = jnp.where(head1, it["bcol"][1], it["bcol"][0])
            it["decay_p"] = jnp.exp(jnp.where(w["incl"], gcol_p - grow_p, NEG_BIG))
        yield
        for it in items:
            it["a"] = jnp.where(it["w"]["strict"], it["beta_p"] * it.pop("kk") * it["decay_p"], 0.0)
            it["qkm"] = q_scale * it.pop("qk") * it.pop("decay_p")
        yield

    def invert(items):
        for it in items:
            it["x"] = _pair_matmul(it["a"], it["a"])
            it["t"] = eye_p - it["a"]
        yield
        for _ in range(int(math.log2(c)) - 3):
            for it in items:
                r2 = _pair_matmul(jnp.concatenate([it["x"], it["t"]], axis=0), it["x"])
                it["x"] = r2[:c]
                it["t"] = it["t"] + r2[c:]
            yield
        for it in items:
            it["t"] = it["t"] + _pair_matmul(it["t"], it["x"])
        yield
        for it in items:
            it["res"] = eye_p - it["t"] - _pair_matmul(it["a"], it["t"])
        yield
        for it in items:
            it["t"] = it["t"] + _pair_matmul(it["t"], it.pop("res"))
        yield

    def chunk_products(items):
        for it in items:
            rhs = []
            for s in range(rep):
                eg = jnp.exp(it["gcol"][s])
                v0 = it["w"]["v_col0"] + s * HEAD
                vf = it["w"]["v_ref"][it["rows"], v0:v0 + HEAD].astype(F32)
                rhs.append(jnp.concatenate([vf * it["bcol"][s], it["kf"] * (it["bcol"][s] * eg)], axis=1))
            it["rhs"] = jnp.concatenate(rhs, axis=0).astype(BF16)
            it["k_dec"] = [(it["kf"] * jnp.exp(it["gtot"][s] - it["gcol"][s])).astype(BF16) for s in range(rep)]
            it["qd"] = [it["qf"] * (q_scale * jnp.exp(it["gcol"][s])) for s in range(rep)]
            it["e_tot"] = [jnp.exp(it["gtot"][s]) for s in range(rep)]
        yield
        for it in items:
            it["uw"] = _dot(_pair_diag(it["t"]), it.pop("rhs"))
        yield
        for it in items:
            it["qo"] = _dot(_pair_diag(it["qkm"]), it["uw"])
        yield
        for it in items:
            it["gn"] = [_dot_tn(it["k_dec"][s], it["uw"][s * c:(s + 1) * c])
                        for s in range(rep)]
        yield
        for it in items:
            it["q_eff"] = [it["qd"][s] - it["qo"][s * c:(s + 1) * c, HEAD:] for s in range(rep)]
        yield

    def advance(items, n_walks):
        for first in range(0, len(items), n_walks):
            for it in items[first:first + n_walks]:
                state = it["w"]["state"]
                for s in range(rep):
                    rs = _dot(jnp.concatenate([it["gn"][s][:, HEAD:], it["q_eff"][s]], axis=0), state[s])
                    state[s] = state[s] * it["e_tot"][s] - rs[:HEAD] + it["gn"][s][:, :HEAD]
                    v0 = it["w"]["v_col0"] + s * HEAD
                    it["w"]["o_ref"][it["rows"], v0:v0 + HEAD] = (
                        rs[HEAD:] + it["qo"][s * c:(s + 1) * c, :HEAD]).astype(BF16)
            yield

    def emit(*stages):
        live = list(stages)
        while live:
            for g in list(live):
                if next(g, live) is live:
                    live.remove(g)

    for w in walks:
        w["state"] = [s_ref[w["reverse"], w["hb"], s] for s in range(rep)]
    items = []
    emit(prepare([(w, w["order"][rank]) for rank in range(n_chunks) for w in walks], items))
    emit(invert(items))
    emit(chunk_products(items))
    emit(advance(items, len(walks)))
    for w in walks:
        for s in range(rep):
            s_ref[w["reverse"], w["hb"], s] = w["state"][s]


def _delta_rule(qkv, gate_rows):
    b, lt, _ = qkv.shape
    nt = lt // ROW_TILE
    rep = DN_V_HEADS // DN_QK_HEADS
    v_col0 = 2 * DN_KEY_DIM // (DN_HEAD_BLOCK * rep * HEAD)
    fwd = lambda r: r
    bwd = lambda r: jnp.where(r == 0, 0, nt - r)

    hb = DN_HEAD_BLOCK
    n_blocks = DN_QK_HEADS // hb

    def specs(tile, d):
        return [
            pl.BlockSpec((None, ROW_TILE, hb * HEAD), lambda bi, j, r: (bi, tile(r), j)),
            pl.BlockSpec((None, ROW_TILE, hb * HEAD), lambda bi, j, r: (bi, tile(r), n_blocks + j)),
            pl.BlockSpec((None, ROW_TILE, hb * rep * HEAD), lambda bi, j, r: (bi, tile(r), v_col0 + j)),
            pl.BlockSpec((None, None, hb, 2 * rep, ROW_TILE), lambda bi, j, r: (bi, d, j, 0, tile(r))),
        ]

    out = lambda tile: pl.BlockSpec((None, ROW_TILE, hb * rep * HEAD), lambda bi, j, r: (bi, tile(r), j))
    o_shape = jax.ShapeDtypeStruct((b, lt, DN_VAL_DIM), BF16)
    return pl.pallas_call(
        _delta_body,
        grid=(b, n_blocks, nt),
        in_specs=specs(fwd, 0) + specs(bwd, 1),
        out_specs=[out(fwd), out(bwd)],
        out_shape=[o_shape, o_shape],
        scratch_shapes=[pltpu.VMEM((2, hb, rep, HEAD, HEAD), F32)],
        compiler_params=_cparams("arbitrary", "arbitrary", "arbitrary"),
        name="dn_delta_rule",
    )(qkv, qkv, qkv, gate_rows, qkv, qkv, qkv, gate_rows)


def _outproj_body(*refs, n_o, n_h, skip, head_norm, final_norm):
    o_refs = refs[:n_o]
    z_ref = refs[n_o]
    h_refs = refs[n_o + 1:n_o + 1 + n_h]
    mod_ref, w_ref = refs[n_o + 1 + n_h:n_o + 3 + n_h]
    rest = list(refs[n_o + 3 + n_h:])
    og_ref = rest.pop(0) if head_norm else None
    fg_ref = rest.pop(0) if final_norm else None
    out_ref = rest.pop(0)
    d = h_refs[0].shape[-1]
    width = z_ref.shape[-1]
    parts = []
    for hb in range(width // HEAD):
        cs = slice(hb * HEAD, (hb + 1) * HEAD)
        o = o_refs[0][:, cs].astype(F32)
        for extra in o_refs[1:]:
            o = o + extra[:, cs].astype(F32)
        if head_norm:
            o = o * lax.rsqrt(jnp.mean(o * o, axis=-1, keepdims=True) + NORM_EPS) * og_ref[...]
        parts.append((o * _silu(z_ref[:, cs].astype(F32))).astype(BF16))
    y = jnp.dot(jnp.concatenate(parts, axis=1), w_ref[...], preferred_element_type=F32)
    hn = _stream_rows(h_refs, skip) + mod_ref[:, 2 * d:3 * d] * y
    if final_norm:
        hn = hn * lax.rsqrt(jnp.mean(hn * hn, axis=-1, keepdims=True) + NORM_EPS) * fg_ref[...]
    out_ref[...] = hn


def _outproj(o_list, z, hs, mod4, w_out, layer, head_g=None, final_g=None):
    b, _, d = hs[0].shape
    lt = sum(a.shape[1] for a in hs)
    nt = lt // ROW_TILE
    d3 = mod4.shape[-1]
    width = z.shape[-1]
    skip = 1 if final_g is not None else 0
    row = lambda c: pl.BlockSpec((None, ROW_TILE, c), lambda bi, r: (bi, r + skip, 0))
    mod_spec = pl.BlockSpec((None, None, 1, d3),
                            lambda bi, r: (layer, jnp.where(r + skip == 0, b, bi), 0, 0))
    in_specs = ([row(width)] * len(o_list) + [row(width)] + _stream_specs(hs, skip)
                + [mod_spec, _resident(w_out.shape)])
    args = list(o_list) + [z, *hs, mod4, w_out]
    if head_g is not None:
        in_specs.append(_resident((1, HEAD)))
        args.append(head_g)
    if final_g is not None:
        in_specs.append(_resident((1, d)))
        args.append(final_g)
    return pl.pallas_call(
        functools.partial(_outproj_body, n_o=len(o_list), n_h=len(hs), skip=skip,
                          head_norm=head_g is not None, final_norm=final_g is not None),
        grid=(b, nt - skip),
        in_specs=in_specs,
        out_specs=pl.BlockSpec((None, ROW_TILE, d), lambda bi, r: (bi, r, 0)),
        out_shape=jax.ShapeDtypeStruct((b, lt - skip * ROW_TILE, d), F32),
        compiler_params=_cparams("arbitrary", "arbitrary"),
        name="mixer_outproj",
    )(*args)


def _rope_partner(x):
    lane = lax.broadcasted_iota(jnp.int32, x.shape, 1)
    quarter = HEAD // 4
    return jnp.where((lane % (2 * quarter)) < quarter,
                     pltpu.roll(x, HEAD - quarter, axis=1), pltpu.roll(x, quarter, axis=1))


def _att_inproj_body(h_ref, mod_ref, g_ref, w_ref, wvt_ref, qg_ref, kg_ref, cos_ref, sin_ref,
                     q_ref, k_ref, vt_ref, z_ref):
    d = h_ref.shape[-1]
    u = _modulated_norm(h_ref[...], g_ref[...], mod_ref[...], d).astype(BF16)
    cos = cos_ref[...]
    sin = sin_ref[...]
    q_scale = (HEAD ** -0.5) * math.log2(math.e)

    def normed_rope(x, g):
        xn = x * lax.rsqrt(jnp.mean(x * x, axis=-1, keepdims=True) + NORM_EPS) * g
        return xn * cos + _rope_partner(xn) * sin

    pair = 2 * HEAD
    for pb in range(ATT_Q_DIM // pair):
        x2 = jnp.dot(u, w_ref[:, pb * pair:(pb + 1) * pair], preferred_element_type=F32)
        for t in range(2):
            cs = slice(pb * pair + t * HEAD, pb * pair + (t + 1) * HEAD)
            q_ref[:, cs] = (normed_rope(x2[:, t * HEAD:(t + 1) * HEAD], qg_ref[...]) * q_scale).astype(BF16)
    for pb in range(ATT_KV_DIM // pair):
        x2 = jnp.dot(u, w_ref[:, ATT_Q_DIM + pb * pair:ATT_Q_DIM + (pb + 1) * pair],
                     preferred_element_type=F32)
        for t in range(2):
            cs = slice(pb * pair + t * HEAD, pb * pair + (t + 1) * HEAD)
            k_ref[:, cs] = normed_rope(x2[:, t * HEAD:(t + 1) * HEAD], kg_ref[...]).astype(BF16)
    vt = lax.dot_general(wvt_ref[...], u, (((1,), (1,)), ((), ())), preferred_element_type=F32)
    for hk in range(ATT_KV_HEADS):
        vt_ref[hk] = vt[hk * HEAD:(hk + 1) * HEAD, :].astype(BF16)
    z0 = ATT_Q_DIM + 2 * ATT_KV_DIM
    for n in range(ATT_Q_DIM // N_COL_CHUNK):
        cs = slice(n * N_COL_CHUNK, (n + 1) * N_COL_CHUNK)
        ws = slice(z0 + n * N_COL_CHUNK, z0 + (n + 1) * N_COL_CHUNK)
        z_ref[:, cs] = jnp.dot(u, w_ref[:, ws], preferred_element_type=F32).astype(BF16)


def _att_inproj(h, mod4, norm_g, w_in, w_vt, q_g, k_g, cos_t, sin_t, layer):
    b, lt, d = h.shape
    nt = lt // ROW_TILE
    d3 = mod4.shape[-1]
    row = lambda c: pl.BlockSpec((None, ROW_TILE, c), lambda bi, r: (bi, r, 0))
    tab = pl.BlockSpec((ROW_TILE, HEAD), lambda bi, r: (r, 0))
    return pl.pallas_call(
        _att_inproj_body,
        grid=(b, nt),
        in_specs=[row(d), _mod_spec(layer, b, d3), _resident((1, d)), _resident(w_in.shape),
                  _resident(w_vt.shape), _resident((1, HEAD)), _resident((1, HEAD)), tab, tab],
        out_specs=[row(ATT_Q_DIM), row(ATT_KV_DIM),
                   pl.BlockSpec((None, ATT_KV_HEADS, None, HEAD, ROW_TILE), lambda bi, r: (bi, 0, r, 0, 0)),
                   row(ATT_Q_DIM)],
        out_shape=[jax.ShapeDtypeStruct((b, lt, ATT_Q_DIM), BF16),
                   jax.ShapeDtypeStruct((b, lt, ATT_KV_DIM), BF16),
                   jax.ShapeDtypeStruct((b, ATT_KV_HEADS, nt, HEAD, ROW_TILE), BF16),
                   jax.ShapeDtypeStruct((b, lt, ATT_Q_DIM), BF16)],
        compiler_params=_cparams("arbitrary", "arbitrary"),
        name="att_inproj",
    )(h, mod4, norm_g, w_in, w_vt, q_g, k_g, cos_t, sin_t)


def _attn_body(q_ref, k_ref, vt_ref, o_ref, s_ref):
    sub = 8
    fold = lambda x, op: op(x.reshape(ROW_TILE // sub, sub, ROW_TILE), axis=0)

    def score_chunk(g, m, chunk):
        hk = g // ATT_GROUP
        st = lax.dot_general(k_ref[chunk * ROW_TILE:(chunk + 1) * ROW_TILE, hk * HEAD:(hk + 1) * HEAD],
                             q_ref[:, g * HEAD:(g + 1) * HEAD],
                             (((1,), (1,)), ((), ())), preferred_element_type=F32)
        s_ref[g % 2, chunk] = st
        return jnp.maximum(m, fold(st, jnp.max))

    def value_chunk(g, m_row, l, acc, chunk):
        pt = jnp.exp2(s_ref[g % 2, chunk] - m_row)
        l = l + fold(pt, jnp.sum)
        return l, acc + jnp.dot(vt_ref[g // ATT_GROUP, chunk], pt.astype(BF16),
                                preferred_element_type=F32)

    def phase(n_chunks, g_score, g_value, m_row):
        m = jnp.full((sub, ROW_TILE), NEG_BIG, F32)
        l = jnp.zeros((sub, ROW_TILE), F32)
        acc = jnp.zeros((HEAD, ROW_TILE), F32)
        for chunk in range(n_chunks):
            if g_score is not None:
                m = score_chunk(g_score, m, chunk)
            if g_value is not None:
                l, acc = value_chunk(g_value, m_row, l, acc, chunk)
        if g_value is not None:
            o_t = acc / jnp.sum(l, axis=0, keepdims=True)
            o_ref[:, g_value * HEAD:(g_value + 1) * HEAD] = o_t.T.astype(BF16)
        return jnp.max(m, axis=0, keepdims=True) if g_score is not None else None

    def attend(n_chunks):
        m_row = phase(n_chunks, 0, None, None)
        for g in range(ATT_Q_HEADS):
            m_row = phase(n_chunks, g + 1 if g + 1 < ATT_Q_HEADS else None, g, m_row)

    is_ctx = pl.program_id(1) == 0
    pl.when(is_ctx)(functools.partial(attend, 1))
    pl.when(jnp.logical_not(is_ctx))(functools.partial(attend, k_ref.shape[0] // ROW_TILE))


def _attention(q, k, vt):
    b, lt, _ = q.shape
    nt = lt // ROW_TILE
    return pl.pallas_call(
        _attn_body,
        grid=(b, nt),
        in_specs=[
            pl.BlockSpec((None, ROW_TILE, ATT_Q_DIM), lambda bi, r: (bi, r, 0)),
            pl.BlockSpec((None, lt, ATT_KV_DIM), lambda bi, r: (bi, 0, 0)),
            pl.BlockSpec((None, ATT_KV_HEADS, nt, HEAD, ROW_TILE), lambda bi, r: (bi, 0, 0, 0, 0)),
        ],
        out_specs=pl.BlockSpec((None, ROW_TILE, ATT_Q_DIM), lambda bi, r: (bi, r, 0)),
        out_shape=jax.ShapeDtypeStruct((b, lt, ATT_Q_DIM), BF16),
        scratch_shapes=[pltpu.VMEM((2, nt, ROW_TILE, ROW_TILE), F32)],
        compiler_params=_cparams("arbitrary", "arbitrary"),
        name="gqa_attention",
    )(q, k, vt)


def _rope_tables(n_lat, n_ctx):
    t = jnp.arange(n_lat)
    row = (t // GRID_W).astype(F32)
    col = (t % GRID_W).astype(F32)
    axis_dim = HEAD // 2
    inv = ROPE_THETA ** (-jnp.arange(0, axis_dim, 2, dtype=F32) / axis_dim)
    ang_r = row[:, None] * inv
    ang_c = col[:, None] * inv
    cos = jnp.concatenate([jnp.cos(ang_r)] * 2 + [jnp.cos(ang_c)] * 2, axis=-1)
    sin = jnp.concatenate([-jnp.sin(ang_r), jnp.sin(ang_r), -jnp.sin(ang_c), jnp.sin(ang_c)], axis=-1)
    cos = jnp.concatenate([jnp.ones((n_ctx, HEAD), F32), cos], axis=0)
    sin = jnp.concatenate([jnp.zeros((n_ctx, HEAD), F32), sin], axis=0)
    return cos, sin


def _lane_pad(vec, offset):
    out = jnp.zeros((1, LANES), F32)
    return lax.dynamic_update_slice(out, vec.reshape(1, -1).astype(F32), (0, offset))


def kernel(x, c, ctx, c_ctx, norm_g, ada_w, ada_b, dn_w_in, dn_conv_w, dn_a_log, dn_dt_bias,
           dn_o_norm_g, dn_w_out, att_w_in, att_q_norm_g, att_k_norm_g, att_w_out, final_norm_g):
    b, n_lat, d = x.shape
    n_ctx = ctx.shape[1]
    depth = norm_g.shape[0]
    assert n_ctx == ROW_TILE and n_lat % ROW_TILE == 0 and b < MOD_ROWS and d % LANES == 0
    lt = n_ctx + n_lat
    rep = DN_V_HEADS // DN_QK_HEADS

    cvec = jnp.concatenate([c, c_ctx[None, :], jnp.zeros((MOD_ROWS - b - 1, d), F32)], axis=0)
    mod4 = _modulation(cvec, ada_w, ada_b).reshape(depth, MOD_ROWS, 1, 3 * d)
    hs = (ctx, x)
    cos_t, sin_t = _rope_tables(n_lat, n_ctx)

    for i in range(depth):
        j = i // 2
        last = i == depth - 1
        g_i = norm_g[i].reshape(1, d)
        if i % 2 == 0:
            w_in = dn_w_in[j].astype(BF16)
            w_main = w_in[:, :DN_CONV_DIM + DN_VAL_DIM]
            w_ab = jnp.pad(w_in[:, DN_CONV_DIM + DN_VAL_DIM:], ((0, 0), (0, LANES - 4 * DN_V_HEADS)))
            qkv_pre, z, ab = _dn_inproj(hs, mod4, g_i, w_main, w_ab, i)
            alog_vec = _lane_pad(dn_a_log[j], 2 * DN_V_HEADS)
            dtb_vec = _lane_pad(dn_dt_bias[j], 2 * DN_V_HEADS)
            qkv, gates = _dn_conv(qkv_pre, ab, dn_conv_w[j], alog_vec, dtb_vec)
            gt = gates[:, :, :4 * DN_V_HEADS].reshape(b, lt, 2, 2, DN_QK_HEADS, rep)
            gate_rows = jnp.transpose(gt[:, :, ::-1], (0, 3, 4, 2, 5, 1)).reshape(
                b, 2, DN_QK_HEADS, 2 * rep, lt)
            o_f, o_b = _delta_rule(qkv, gate_rows)
            hs = (_outproj([o_f, o_b], z, hs, mod4, dn_w_out[j].astype(BF16), i,
                           head_g=dn_o_norm_g[j].reshape(1, HEAD),
                           final_g=final_norm_g.reshape(1, d) if last else None),)
        else:
            if len(hs) > 1:
                hs = (jnp.concatenate(hs, axis=1),)
            w_in = att_w_in[j].astype(BF16)
            w_vt = w_in[:, ATT_Q_DIM + ATT_KV_DIM:ATT_Q_DIM + 2 * ATT_KV_DIM].T
            q, k, vt, z = _att_inproj(hs[0], mod4, g_i, w_in, w_vt,
                                      att_q_norm_g[j].reshape(1, HEAD), att_k_norm_g[j].reshape(1, HEAD),
                                      cos_t, sin_t, i)
            o = _attention(q, k, vt)
            hs = (_outproj([o], z, hs, mod4, att_w_out[j].astype(BF16), i,
                           final_g=final_norm_g.reshape(1, d) if last else None),)
    return hs[0]
```
